```python
import math
import jax, jax.numpy as jnp
from jax import lax
import numpy as np

D_MODEL = 2048
BATCH = 2
SEQ = 8192
DEPTH = 1

HEAD_DIM = 128
N_HEADS_SB = D_MODEL // (2 * HEAD_DIM)
N_HEADS_MOBA = D_MODEL // (2 * HEAD_DIM)
SB_WIDTH = N_HEADS_SB * HEAD_DIM
MOBA_WIDTH = N_HEADS_MOBA * HEAD_DIM
MIX_WIDTH = SB_WIDTH + MOBA_WIDTH
SB_Q_BLOCK = 128
MOBA_BLOCK = 256
MOBA_TOPK = 3
MOBA_Q_CHUNK = 32
N_EXPERTS = 32
TOP_K = 4
D_FF = D_MODEL
SWIGLU_LIMIT = 7.0
SWIGLU_ALPHA = 1.702
EXPERT_ROW_BLOCK = 256
PLE_DIM = 256
LN_EPS = 1e-5
RMS_EPS = 1e-6
NEG_INF = -1e30
DEEPNORM_ALPHA = (2.0 * DEPTH) ** 0.25
DEEPNORM_BETA = (8.0 * DEPTH) ** -0.25

kernel_name = 'hybrid_stickbreak_moba_moe_deepnorm'


def alibi_slopes(n):
    return jnp.asarray(2.0 ** (-8.0 * np.arange(1, n + 1) / n), dtype=jnp.float32)


def layer_norm(x, g, b):
    xf = x.astype(jnp.float32)
    mu = jnp.mean(xf, axis=-1, keepdims=True)
    var = jnp.mean(jnp.square(xf - mu), axis=-1, keepdims=True)
    y = (xf - mu) * lax.rsqrt(var + LN_EPS)
    return (y * g + b).astype(x.dtype)


def head_rms_norm(o, gain):
    b, h, s, d = o.shape
    of = o.astype(jnp.float32)
    y = of * lax.rsqrt(jnp.mean(jnp.square(of), axis=-1, keepdims=True) + RMS_EPS)
    y = y.transpose(0, 2, 1, 3).reshape(b, s, h * d)
    return (y * gain).astype(o.dtype)


def to_heads(t, n_heads):
    b, s, _ = t.shape
    return t.reshape(b, s, n_heads, HEAD_DIM).transpose(0, 2, 1, 3)


def stick_breaking_attention(q, k, v):
    b, h, s, d = q.shape
    n_blk = s // SB_Q_BLOCK
    scale = 1.0 / math.sqrt(d)
    key_pos = jnp.arange(s)

    def one_block(i):
        q0 = i * SB_Q_BLOCK
        qb = lax.dynamic_slice_in_dim(q, q0, SB_Q_BLOCK, axis=2)
        z = jnp.einsum('bhqd,bhkd->bhqk', qb, k).astype(jnp.float32) * scale
        qpos = q0 + jnp.arange(SB_Q_BLOCK)
        past = key_pos[None, :] < qpos[:, None]
        log_keep = jnp.where(past, jax.nn.log_sigmoid(-z), 0.0)
        between = lax.cumsum(log_keep, axis=3, reverse=True) - log_keep
        w = jnp.where(past, jnp.exp(jax.nn.log_sigmoid(z) + between), 0.0)
        return jnp.einsum('bhqk,bhkd->bhqd', w.astype(v.dtype), v)

    out = lax.map(one_block, jnp.arange(n_blk))
    return out.transpose(1, 2, 0, 3, 4).reshape(b, h, s, d)


def moba_attention(q, k, v, slopes):
    b, h, s, d = q.shape
    n_kb = -(-s // MOBA_BLOCK)
    s_pad = n_kb * MOBA_BLOCK
    pad = [(0, 0), (0, 0), (0, s_pad - s), (0, 0)]
    q = jnp.pad(q, pad)
    k = jnp.pad(k, pad)
    v = jnp.pad(v, pad)
    top = min(MOBA_TOPK, n_kb)
    scale = 1.0 / math.sqrt(d)
    k_blocks = k.reshape(b, h, n_kb, MOBA_BLOCK, d)
    v_blocks = v.reshape(b, h, n_kb, MOBA_BLOCK, d)
    k_mean = jnp.mean(k_blocks.astype(jnp.float32), axis=3).astype(k.dtype)
    blk_off = jnp.arange(MOBA_BLOCK)
    blk_ids = jnp.arange(n_kb)
    m = slopes.reshape(1, h, 1, 1)
    take = jax.vmap(jax.vmap(lambda blocks, idx: blocks[idx]))

    def one_chunk(i):
        q0 = i * MOBA_Q_CHUNK
        own = q0 // MOBA_BLOCK
        qc = lax.dynamic_slice_in_dim(q, q0, MOBA_Q_CHUNK, axis=2)
        qpos = q0 + jnp.arange(MOBA_Q_CHUNK)
        gate = jnp.einsum('bhqd,bhnd->bhqn', qc, k_mean).astype(jnp.float32)
        gate = jnp.where(blk_ids < own, gate, -jnp.inf)
        _, idx = lax.top_k(gate, top)
        sel_valid = idx < own
        k_sel = take(k_blocks, idx)
        v_sel = take(v_blocks, idx)
        s_sel = jnp.einsum('bhqd,bhqnld->bhqnl', qc, k_sel).astype(jnp.float32) * scale
        pos_sel = idx[..., None] * MOBA_BLOCK + blk_off
        dist_sel = (qpos[:, None, None] - pos_sel).astype(jnp.float32)
        s_sel = s_sel - m[..., None] * dist_sel
        s_sel = jnp.where(sel_valid[..., None], s_sel, NEG_INF)
        k_own = lax.dynamic_slice_in_dim(k, own * MOBA_BLOCK, MOBA_BLOCK, axis=2)
        v_own = lax.dynamic_slice_in_dim(v, own * MOBA_BLOCK, MOBA_BLOCK, axis=2)
        own_pos = own * MOBA_BLOCK + blk_off
        dist_own = (qpos[:, None] - own_pos[None, :]).astype(jnp.float32)
        s_own = jnp.einsum('bhqd,bhld->bhql', qc, k_own).astype(jnp.float32) * scale - m * dist_own
        s_own = jnp.where(own_pos[None, :] <= qpos[:, None], s_own, NEG_INF)
        logits = jnp.concatenate([s_sel.reshape(b, h, MOBA_Q_CHUNK, top * MOBA_BLOCK), s_own], axis=-1)
        probs = jax.nn.softmax(logits, axis=-1).astype(v.dtype)
        p_sel = probs[..., :top * MOBA_BLOCK].reshape(b, h, MOBA_Q_CHUNK, top, MOBA_BLOCK)
        p_own = probs[..., top * MOBA_BLOCK:]
        return (jnp.einsum('bhqnl,bhqnld->bhqd', p_sel, v_sel)
                + jnp.einsum('bhql,bhld->bhqd', p_own, v_own))

    out = lax.map(one_chunk, jnp.arange(s_pad // MOBA_Q_CHUNK))
    out = out.transpose(1, 2, 0, 3, 4).reshape(b, h, s_pad, d)
    return out[:, :, :s]


def moe_ffn(x, w_router, b_router, w_up, b_up, w_down, b_down):
    b, s, d = x.shape
    xf = x.reshape(-1, d)
    n = xf.shape[0]
    logits = (xf @ w_router + b_router).astype(jnp.float32)
    top_vals, top_idx = lax.top_k(logits, TOP_K)
    gates = jax.nn.softmax(top_vals, axis=-1).astype(x.dtype)
    n_assign = n * TOP_K
    flat_e = top_idx.reshape(-1)
    flat_tok = jnp.repeat(jnp.arange(n, dtype=jnp.int32), TOP_K)
    flat_g = gates.reshape(-1)
    order = jnp.argsort(flat_e)
    e_sorted = flat_e[order]
    counts = jnp.bincount(flat_e, length=N_EXPERTS)
    padded = (counts + EXPERT_ROW_BLOCK - 1) // EXPERT_ROW_BLOCK * EXPERT_ROW_BLOCK
    pad_end = jnp.cumsum(padded)
    pad_start = pad_end - padded
    start = jnp.cumsum(counts) - counts
    dest = pad_start[e_sorted] + (jnp.arange(n_assign) - start[e_sorted])
    n_blocks = -(-n_assign // EXPERT_ROW_BLOCK) + N_EXPERTS
    n_rows = n_blocks * EXPERT_ROW_BLOCK
    row_tok = jnp.full((n_rows,), n, dtype=jnp.int32).at[dest].set(flat_tok[order])
    row_gate = jnp.zeros((n_rows,), dtype=x.dtype).at[dest].set(flat_g[order])
    block_e = jnp.minimum(
        jnp.searchsorted(pad_end, jnp.arange(n_blocks) * EXPERT_ROW_BLOCK, side='right'),
        N_EXPERTS - 1)
    x_ext = jnp.concatenate([xf, jnp.zeros((1, d), xf.dtype)], axis=0)

    def expert_block(args):
        tok, gate, e = args
        xb = x_ext[tok]
        hdn = xb @ w_up[e] + b_up[e]
        glu = jnp.minimum(hdn[:, 0::2], SWIGLU_LIMIT)
        lin = jnp.clip(hdn[:, 1::2], -SWIGLU_LIMIT, SWIGLU_LIMIT)
        act = glu * jax.nn.sigmoid(SWIGLU_ALPHA * glu) * (lin + 1.0)
        return (act @ w_down[e] + b_down[e]) * gate[:, None]

    y = lax.map(expert_block, (row_tok.reshape(n_blocks, EXPERT_ROW_BLOCK),
                               row_gate.reshape(n_blocks, EXPERT_ROW_BLOCK), block_e))
    out = jnp.zeros((n + 1, d), dtype=x.dtype).at[row_tok].add(y.reshape(n_rows, d))[:n]
    return out.reshape(b, s, d)


def setup_inputs(seed: int = 0) -> dict:
    key = jax.random.key(seed)
    ks = jax.random.split(key, 20)
    f32 = jnp.float32
    beta = DEEPNORM_BETA

    def nrm(k, shape, scale):
        return jax.random.normal(k, shape, f32) * scale

    col_scale = np.ones((3 * MIX_WIDTH,), np.float32)
    col_scale[2 * SB_WIDTH:3 * SB_WIDTH] = beta
    col_scale[3 * SB_WIDTH + 2 * MOBA_WIDTH:] = beta
    x = nrm(ks[0], (BATCH, SEQ, D_MODEL), 1.0)
    p = nrm(ks[1], (DEPTH, BATCH, SEQ, PLE_DIM), 1.0)
    w_in = nrm(ks[2], (DEPTH, D_MODEL, 3 * MIX_WIDTH), D_MODEL ** -0.5) * jnp.asarray(col_scale)
    norm_sb = 1.0 + nrm(ks[3], (DEPTH, SB_WIDTH), 0.02)
    norm_moba = 1.0 + nrm(ks[4], (DEPTH, MOBA_WIDTH), 0.02)
    w_out = nrm(ks[5], (DEPTH, MIX_WIDTH, D_MODEL), MIX_WIDTH ** -0.5 * beta)
    ln1_g = 1.0 + nrm(ks[6], (DEPTH, D_MODEL), 0.02)
    ln1_b = nrm(ks[7], (DEPTH, D_MODEL), 0.02)
    w_router = nrm(ks[8], (DEPTH, D_MODEL, N_EXPERTS), D_MODEL ** -0.5)
    b_router = nrm(ks[9], (DEPTH, N_EXPERTS), 0.01)
    w_up = nrm(ks[10], (DEPTH, N_EXPERTS, D_MODEL, 2 * D_FF), D_MODEL ** -0.5 * beta)
    b_up = nrm(ks[11], (DEPTH, N_EXPERTS, 2 * D_FF), 0.01)
    w_down = nrm(ks[12], (DEPTH, N_EXPERTS, D_FF, D_MODEL), D_FF ** -0.5 * beta)
    b_down = nrm(ks[13], (DEPTH, N_EXPERTS, D_MODEL), 0.01)
    ln2_g = 1.0 + nrm(ks[14], (DEPTH, D_MODEL), 0.02)
    ln2_b = nrm(ks[15], (DEPTH, D_MODEL), 0.02)
    w_ple = nrm(ks[16], (DEPTH, PLE_DIM, D_MODEL), PLE_DIM ** -0.5)
    w_ple_gate = nrm(ks[17], (DEPTH, D_MODEL, D_MODEL), D_MODEL ** -0.5)
    b_ple_gate = nrm(ks[18], (DEPTH, D_MODEL), 0.01)
    return {'x': x, 'p': p, 'w_in': w_in, 'norm_sb': norm_sb, 'norm_moba': norm_moba,
            'w_out': w_out, 'ln1_g': ln1_g, 'ln1_b': ln1_b, 'w_router': w_router,
            'b_router': b_router, 'w_up': w_up, 'b_up': b_up, 'w_down': w_down,
            'b_down': b_down, 'ln2_g': ln2_g, 'ln2_b': ln2_b, 'w_ple': w_ple,
            'w_ple_gate': w_ple_gate, 'b_ple_gate': b_ple_gate}


def reference(x, p, w_in, norm_sb, norm_moba, w_out, ln1_g, ln1_b, w_router, b_router,
              w_up, b_up, w_down, b_down, ln2_g, ln2_b, w_ple, w_ple_gate, b_ple_gate):
    slopes = alibi_slopes(N_HEADS_MOBA)
    splits = [SB_WIDTH, 2 * SB_WIDTH, 3 * SB_WIDTH,
              3 * SB_WIDTH + MOBA_WIDTH, 3 * SB_WIDTH + 2 * MOBA_WIDTH]
    for i in range(DEPTH):
        proj = x @ w_in[i]
        q_sb, k_sb, v_sb, q_mb, k_mb, v_mb = jnp.split(proj, splits, axis=-1)
        o_sb = stick_breaking_attention(to_heads(q_sb, N_HEADS_SB), to_heads(k_sb, N_HEADS_SB),
                                        to_heads(v_sb, N_HEADS_SB))
        o_mb = moba_attention(to_heads(q_mb, N_HEADS_MOBA), to_heads(k_mb, N_HEADS_MOBA),
                              to_heads(v_mb, N_HEADS_MOBA), slopes)
        mixed = jnp.concatenate([head_rms_norm(o_sb, norm_sb[i]),
                                 head_rms_norm(o_mb, norm_moba[i])], axis=-1)
        y = mixed @ w_out[i]
        x = layer_norm(DEEPNORM_ALPHA * x + y, ln1_g[i], ln1_b[i])
        f = moe_ffn(x, w_router[i], b_router[i], w_up[i], b_up[i], w_down[i], b_down[i])
        x = layer_norm(DEEPNORM_ALPHA * x + f, ln2_g[i], ln2_b[i])
        x = x + jax.nn.sigmoid(x @ w_ple_gate[i] + b_ple_gate[i]) * (p[i] @ w_ple[i])
    return x
```

```python
import functools
import math

import jax
import jax.numpy as jnp
import numpy as np
from jax import lax
from jax.experimental import pallas as pl
from jax.experimental.pallas import tpu as pltpu

HEAD_DIM = 128
N_HEADS_SB = 8
N_HEADS_MOBA = 8
MOBA_BLOCK = 256
MOBA_TOPK = 3
N_EXPERTS = 32
TOP_K = 4
SWIGLU_LIMIT = 7.0
SWIGLU_ALPHA = 1.702
LN_EPS = 1e-5
RMS_EPS = 1e-6
NEG_INF = -1e30
DEPTH = 1
DEEPNORM_ALPHA = (2.0 * DEPTH) ** 0.25

ATT_TILE = 256
ROW_BLOCK = 256
TOKEN_TILE = 256
PROJ_TM = 1024
PROJ_TN = 1024
EXPERT_TN = 1024
MXU_DIM = 256
MIB = 1024 * 1024

F32 = jnp.float32
BF16 = jnp.bfloat16


def _params(semantics, vmem_mib):
    return pltpu.CompilerParams(dimension_semantics=semantics,
                                vmem_limit_bytes=vmem_mib * MIB)


def _split_bf16(a):
    hi = a.astype(BF16)
    lo = (a - hi.astype(F32)).astype(BF16)
    return hi, lo


def _dot(a, b):
    return jnp.dot(a, b, preferred_element_type=F32)


def _dot_nt(a, b):
    return lax.dot_general(a, b, (((1,), (1,)), ((), ())), preferred_element_type=F32)


def _qkv_kernel(x_ref, w_ref, o_ref, xb_ref, *, q_tiles, scale):
    n = pl.program_id(1)

    @pl.when(n == 0)
    def _():
        xb_ref[...] = x_ref[...].astype(BF16)

    acc = _dot(xb_ref[...], w_ref[...])
    is_q = functools.reduce(jnp.logical_or, [n == t for t in q_tiles])
    o_ref[...] = (acc * jnp.where(is_q, scale, 1.0)).astype(o_ref.dtype)


def _qkv_proj(x2d, w_bf16, q_tiles, scale):
    n_tok, d = x2d.shape
    n_out = w_bf16.shape[1]
    tm = min(PROJ_TM, n_tok)
    tn = PROJ_TN
    return pl.pallas_call(
        functools.partial(_qkv_kernel, q_tiles=q_tiles, scale=scale),
        grid=(n_tok // tm, n_out // tn),
        in_specs=[pl.BlockSpec((tm, d), lambda m, n: (m, 0)),
                  pl.BlockSpec((d, tn), lambda m, n: (0, n))],
        out_specs=pl.BlockSpec((tm, tn), lambda m, n: (m, n)),
        out_shape=jax.ShapeDtypeStruct((n_tok, n_out), BF16),
        scratch_shapes=[pltpu.VMEM((tm, d), BF16)],
        compiler_params=_params(("arbitrary", "arbitrary"), 48),
        name="qkv_proj",
    )(x2d, w_bf16)


def _head_rms(o, gain):
    ms = jnp.mean(o * o, axis=-1, keepdims=True)
    return o * lax.rsqrt(ms + RMS_EPS) * gain


def _sb_kernel(q_ref, k_ref, v_ref, u_ref, g_ref, o_ref, *, t):
    qi = pl.program_id(2)
    q = q_ref[...]
    u = u_ref[...]
    row = lax.broadcasted_iota(jnp.int32, (t, t), 0)
    col = lax.broadcasted_iota(jnp.int32, (t, t), 1)
    past = col < row

    def tile(j, c, acc, diag):
        start = pl.multiple_of(j * t, t)
        k = k_ref[pl.ds(start, t), :]
        v = v_ref[pl.ds(start, t), :]
        z = _dot_nt(q, k)
        e = jnp.exp(-jnp.abs(z))
        log_keep = -(jnp.maximum(z, 0.0) + jnp.log(1.0 + e))
        if diag:
            log_keep = jnp.where(past, log_keep, 0.0)
        hi, lo = _split_bf16(log_keep)
        between = _dot(hi, u) + _dot(lo, u)
        w = jnp.exp(z + log_keep + between + c)
        if diag:
            w = jnp.where(past, w, 0.0)
        acc = acc + _dot(w.astype(BF16), v)
        c = c + between[:, :1] + log_keep[:, :1]
        return c, acc

    c0 = jnp.zeros((t, 1), F32)
    acc0 = jnp.zeros((t, HEAD_DIM), F32)
    c, acc = tile(qi, c0, acc0, True)

    def body(it, carry):
        return tile(qi - 1 - it, carry[0], carry[1], False)

    c, acc = lax.fori_loop(0, qi, body, (c, acc))
    o_ref[...] = _head_rms(acc, g_ref[...]).astype(o_ref.dtype)


def _sb_attention(qkv, u, gain, *, q_col, k_col, v_col, n_heads):
    b, s, _ = qkv.shape
    t = ATT_TILE
    return pl.pallas_call(
        functools.partial(_sb_kernel, t=t),
        grid=(b, n_heads, s // t),
        in_specs=[pl.BlockSpec((None, t, HEAD_DIM), lambda b, h, i: (b, i, q_col + h)),
                  pl.BlockSpec((None, s, HEAD_DIM), lambda b, h, i: (b, 0, k_col + h)),
                  pl.BlockSpec((None, s, HEAD_DIM), lambda b, h, i: (b, 0, v_col + h)),
                  pl.BlockSpec((t, t), lambda b, h, i: (0, 0)),
                  pl.BlockSpec((1, HEAD_DIM), lambda b, h, i: (0, h))],
        out_specs=pl.BlockSpec((None, t, HEAD_DIM), lambda b, h, i: (b, i, h)),
        out_shape=jax.ShapeDtypeStruct((b, s, n_heads * HEAD_DIM), BF16),
        compiler_params=_params(("arbitrary", "arbitrary", "arbitrary"), 32),
        name="sb_attention",
    )(qkv, qkv, qkv, u, gain)


def _moba_kernel(slope_ref, q_ref, k_ref, v_ref, avg_ref, dist_ref, g_ref, o_ref, km_ref,
                 *, t, n_blk):
    h = pl.program_id(1)
    qi = pl.program_id(2)

    @pl.when(qi == 0)
    def _():
        km_ref[...] = _dot(avg_ref[...], k_ref[...])

    q = q_ref[...]
    slope = slope_ref[h]
    alibi = slope * dist_ref[...]
    row = lax.broadcasted_iota(jnp.int32, (t, t), 0)
    col = lax.broadcasted_iota(jnp.int32, (t, t), 1)

    km_hi, km_lo = _split_bf16(km_ref[...])
    gate = _dot_nt(q, km_hi) + _dot_nt(q, km_lo)
    blk = lax.broadcasted_iota(jnp.int32, (t, n_blk), 1)
    valid = blk < qi
    work = jnp.where(valid, gate, -jnp.inf)
    sel = jnp.zeros((t, n_blk), jnp.bool_)
    for _ in range(MOBA_TOPK):
        m = jnp.max(work, axis=1, keepdims=True)
        idx = jnp.min(jnp.where(work == m, blk, n_blk), axis=1, keepdims=True)
        hit = blk == idx
        sel = jnp.logical_or(sel, hit)
        work = jnp.where(hit, -jnp.inf, work)
    sel_bias = jnp.where(jnp.logical_and(sel, valid), 0.0, NEG_INF)

    start = pl.multiple_of(qi * t, t)
    k = k_ref[pl.ds(start, t), :]
    v = v_ref[pl.ds(start, t), :]
    s = jnp.where(col <= row, _dot_nt(q, k) + alibi, NEG_INF)
    m0 = jnp.max(s, axis=1, keepdims=True)
    p = jnp.exp(s - m0)
    l0 = jnp.sum(p, axis=1, keepdims=True)
    acc0 = _dot(p.astype(BF16), v)

    def body(j, carry):
        m_run, l_run, acc = carry
        start = pl.multiple_of(j * t, t)
        k = k_ref[pl.ds(start, t), :]
        v = v_ref[pl.ds(start, t), :]
        col_bias = jnp.max(jnp.where(blk == j, sel_bias, -jnp.inf), axis=1, keepdims=True)
        off = slope * ((qi - j) * t).astype(F32)
        s = _dot_nt(q, k) + alibi + (col_bias - off)
        m_new = jnp.maximum(m_run, jnp.max(s, axis=1, keepdims=True))
        a = jnp.exp(m_run - m_new)
        p = jnp.exp(s - m_new)
        l_run = a * l_run + jnp.sum(p, axis=1, keepdims=True)
        acc = a * acc + _dot(p.astype(BF16), v)
        return m_new, l_run, acc

    _, l_run, acc = lax.fori_loop(0, qi, body, (m0, l0, acc0))
    o_ref[...] = _head_rms(acc / l_run, g_ref[...]).astype(o_ref.dtype)


def _moba_attention(qkv, slopes, avg, dist, gain, *, q_col, k_col, v_col, n_heads):
    b, s, _ = qkv.shape
    t = ATT_TILE
    n_blk = s // t
    grid_spec = pltpu.PrefetchScalarGridSpec(
        num_scalar_prefetch=1,
        grid=(b, n_heads, n_blk),
        in_specs=[pl.BlockSpec((None, t, HEAD_DIM), lambda b, h, i, sl: (b, i, q_col + h)),
                  pl.BlockSpec((None, s, HEAD_DIM), lambda b, h, i, sl: (b, 0, k_col + h)),
                  pl.BlockSpec((None, s, HEAD_DIM), lambda b, h, i, sl: (b, 0, v_col + h)),
                  pl.BlockSpec((n_blk, s), lambda b, h, i, sl: (0, 0)),
                  pl.BlockSpec((t, t), lambda b, h, i, sl: (0, 0)),
                  pl.BlockSpec((1, HEAD_DIM), lambda b, h, i, sl: (0, h))],
        out_specs=pl.BlockSpec((None, t, HEAD_DIM), lambda b, h, i, sl: (b, i, h)),
        scratch_shapes=[pltpu.VMEM((n_blk, HEAD_DIM), F32)])
    return pl.pallas_call(
        functools.partial(_moba_kernel, t=t, n_blk=n_blk),
        grid_spec=grid_spec,
        out_shape=jax.ShapeDtypeStruct((b, s, n_heads * HEAD_DIM), BF16),
        compiler_params=_params(("arbitrary", "arbitrary", "arbitrary"), 32),
        name="moba_attention",
    )(slopes, qkv, qkv, qkv, avg, dist, gain)


def _layer_norm(x, g, b):
    mu = jnp.mean(x, axis=-1, keepdims=True)
    xc = x - mu
    var = jnp.mean(xc * xc, axis=-1, keepdims=True)
    return xc * lax.rsqrt(var + LN_EPS) * g + b


def _oproj_router_kernel(ms_ref, mm_ref, x_ref, wo_ref, g_ref, b_ref, wr_ref, br_ref, tri_ref,
                         x1_ref, te_ref, gt_ref, rk_ref, cnt_ref, carry_ref, *, sb_width):
    i = pl.program_id(0)

    @pl.when(i == 0)
    def _():
        carry_ref[...] = jnp.zeros_like(carry_ref)

    y = _dot(ms_ref[...], wo_ref[:sb_width, :]) + _dot(mm_ref[...], wo_ref[sb_width:, :])
    x1 = _layer_norm(DEEPNORM_ALPHA * x_ref[...] + y, g_ref[...], b_ref[...])
    x1_ref[...] = x1

    x_hi, x_lo = _split_bf16(x1)
    w_hi, w_lo = _split_bf16(wr_ref[...])
    logits = _dot(x_hi, w_hi) + _dot(x_hi, w_lo) + _dot(x_lo, w_hi) + br_ref[...]

    n_tok, n_exp = logits.shape
    lane = lax.broadcasted_iota(jnp.int32, (n_tok, n_exp), 1)
    work = logits
    vals, hits = [], []
    for _ in range(TOP_K):
        m = jnp.max(work, axis=1, keepdims=True)
        idx = jnp.min(jnp.where(work == m, lane, n_exp), axis=1, keepdims=True)
        hit = lane == idx
        vals.append(m)
        hits.append(hit)
        work = jnp.where(hit, -jnp.inf, work)
    exps = [jnp.exp(v - vals[0]) for v in vals]
    denom = functools.reduce(jnp.add, exps)

    onehot = functools.reduce(jnp.add, [hit.astype(F32) for hit in hits])
    rank_excl = carry_ref[...] + _dot(tri_ref[...], onehot.astype(BF16))
    carry_ref[...] = carry_ref[...] + jnp.sum(onehot, axis=0, keepdims=True)
    cnt_ref[...] = carry_ref[...]

    te_ref[...] = jnp.concatenate(
        [jnp.sum(jnp.where(hit, lane, 0), axis=1, keepdims=True) for hit in hits], axis=1)
    gt_ref[...] = jnp.concatenate([e / denom for e in exps], axis=1)
    rk_ref[...] = jnp.concatenate(
        [jnp.sum(jnp.where(hit, rank_excl, 0.0), axis=1, keepdims=True) for hit in hits],
        axis=1).astype(jnp.int32)


def _oproj_router(mixed_sb, mixed_mb, x2d, w_out, ln_g, ln_b, w_router, b_router, tri):
    n_tok, d = x2d.shape
    sb_width = mixed_sb.shape[1]
    mb_width = mixed_mb.shape[1]
    n_exp = w_router.shape[1]
    t = TOKEN_TILE
    const = lambda i: (0, 0)
    tok = lambda i: (i, 0)
    return pl.pallas_call(
        functools.partial(_oproj_router_kernel, sb_width=sb_width),
        grid=(n_tok // t,),
        in_specs=[pl.BlockSpec((t, sb_width), tok),
                  pl.BlockSpec((t, mb_width), tok),
                  pl.BlockSpec((t, d), tok),
                  pl.BlockSpec((sb_width + mb_width, d), const),
                  pl.BlockSpec((1, d), const),
                  pl.BlockSpec((1, d), const),
                  pl.BlockSpec((d, n_exp), const),
                  pl.BlockSpec((1, n_exp), const),
                  pl.BlockSpec((t, t), const)],
        out_specs=[pl.BlockSpec((t, d), tok),
                   pl.BlockSpec((t, TOP_K), tok),
                   pl.BlockSpec((t, TOP_K), tok),
                   pl.BlockSpec((t, TOP_K), tok),
                   pl.BlockSpec((1, n_exp), const)],
        out_shape=[jax.ShapeDtypeStruct((n_tok, d), F32),
                   jax.ShapeDtypeStruct((n_tok, TOP_K), jnp.int32),
                   jax.ShapeDtypeStruct((n_tok, TOP_K), F32),
                   jax.ShapeDtypeStruct((n_tok, TOP_K), jnp.int32),
                   jax.ShapeDtypeStruct((1, n_exp), F32)],
        scratch_shapes=[pltpu.VMEM((1, n_exp), F32)],
        compiler_params=_params(("arbitrary",), 48),
        name="oproj_ln_router",
    )(mixed_sb, mixed_mb, x2d, w_out, ln_g, ln_b, w_router, b_router, tri)


def _row_copy(src_hbm, dst_hbm, sem, src_row, dst_row, n_rows=1):
    return pltpu.make_async_copy(src_hbm.at[pl.ds(src_row, n_rows)],
                                 dst_hbm.at[pl.ds(dst_row, n_rows)], sem)


def _sort_gather_kernel(tok_hbm, x_hbm, o_hbm, idx_ref, idx_sem, sems, *, rb):
    b = pl.program_id(0)
    slot = lax.rem(b, 2)
    idx_copy = pltpu.make_async_copy(tok_hbm.at[b], idx_ref, idx_sem)
    idx_copy.start()
    idx_copy.wait()

    def issue(r, carry):
        _row_copy(x_hbm, o_hbm, sems.at[slot], idx_ref[r], b * rb + r).start()
        return carry

    lax.fori_loop(0, rb, issue, 0, unroll=8)

    @pl.when(b > 0)
    def _():
        _row_copy(x_hbm, o_hbm, sems.at[1 - slot], 0, 0, rb).wait()

    @pl.when(b == pl.num_programs(0) - 1)
    def _():
        _row_copy(x_hbm, o_hbm, sems.at[slot], 0, 0, rb).wait()


def _sort_gather(row_tok, x1, n_blocks):
    n_tok, d = x1.shape
    rb = ROW_BLOCK
    return pl.pallas_call(
        functools.partial(_sort_gather_kernel, rb=rb),
        grid=(n_blocks,),
        in_specs=[pl.BlockSpec(memory_space=pl.ANY), pl.BlockSpec(memory_space=pl.ANY)],
        out_specs=pl.BlockSpec(memory_space=pl.ANY),
        out_shape=jax.ShapeDtypeStruct((n_blocks * rb, d), x1.dtype),
        scratch_shapes=[pltpu.SMEM((rb,), jnp.int32),
                        pltpu.SemaphoreType.DMA(()),
                        pltpu.SemaphoreType.DMA((2,))],
        compiler_params=_params(("arbitrary",), 16),
        name="sort_gather",
    )(row_tok, x1)


def _expert_changed(be_ref, b):
    prev = be_ref[jnp.maximum(b - 1, 0)]
    return jnp.logical_or(b == 0, be_ref[b] != prev)


def _expert_up_kernel(be_ref, nused_ref, xs_ref, w_ref, bias_ref, sel_ref, o_ref, wb_ref):
    b = pl.program_id(1)

    @pl.when(b >= nused_ref[0])
    def _():
        o_ref[...] = jnp.zeros_like(o_ref)

    @pl.when(b < nused_ref[0])
    def _():
        @pl.when(_expert_changed(be_ref, b))
        def _():
            wb_ref[...] = w_ref[...].astype(BF16)

        hdn = _dot(xs_ref[...].astype(BF16), wb_ref[...]) + bias_ref[...]
        tn = hdn.shape[1]
        nxt = pltpu.roll(hdn, tn - 1, 1)
        glu = jnp.minimum(hdn, SWIGLU_LIMIT)
        lin = jnp.clip(nxt, -SWIGLU_LIMIT, SWIGLU_LIMIT)
        act = (glu * jax.nn.sigmoid(SWIGLU_ALPHA * glu) * (lin + 1.0)).astype(BF16)
        sel = sel_ref[...]
        o_ref[...] = jnp.concatenate(
            [_dot(act[:, c:c + MXU_DIM], sel) for c in range(0, tn, MXU_DIM)], axis=1).astype(o_ref.dtype)


def _expert_up(block_e, n_used, xs, w_up, b_up, sel, n_blocks):
    n_rows, d = xs.shape
    n_exp, _, f2 = w_up.shape
    rb = ROW_BLOCK
    tn = EXPERT_TN

    def used(b, nu):
        return jnp.minimum(b, nu[0] - 1)

    grid_spec = pltpu.PrefetchScalarGridSpec(
        num_scalar_prefetch=2,
        grid=(f2 // tn, n_blocks),
        in_specs=[pl.BlockSpec((rb, d), lambda n, b, be, nu: (used(b, nu), 0)),
                  pl.BlockSpec((None, d, tn), lambda n, b, be, nu: (be[used(b, nu)], 0, n)),
                  pl.BlockSpec((None, 1, tn), lambda n, b, be, nu: (be[used(b, nu)], 0, n)),
                  pl.BlockSpec((MXU_DIM, MXU_DIM // 2), lambda n, b, be, nu: (0, 0))],
        out_specs=pl.BlockSpec((rb, tn // 2), lambda n, b, be, nu: (b, n)),
        scratch_shapes=[pltpu.VMEM((d, tn), BF16)])
    return pl.pallas_call(
        _expert_up_kernel,
        grid_spec=grid_spec,
        out_shape=jax.ShapeDtypeStruct((n_rows, f2 // 2), BF16),
        compiler_params=_params(("arbitrary", "arbitrary"), 48),
        name="expert_up",
    )(block_e, n_used, xs, w_up, b_up, sel)


def _expert_down_kernel(be_ref, nused_ref, a_ref, w_ref, bias_ref, o_ref, wb_ref):
    b = pl.program_id(1)

    @pl.when(b >= nused_ref[0])
    def _():
        o_ref[...] = jnp.zeros_like(o_ref)

    @pl.when(b < nused_ref[0])
    def _():
        @pl.when(_expert_changed(be_ref, b))
        def _():
            wb_ref[...] = w_ref[...].astype(BF16)

        o_ref[...] = _dot(a_ref[...], wb_ref[...]) + bias_ref[...]


def _expert_down(block_e, n_used, act, w_down, b_down, n_blocks):
    n_exp, f, d = w_down.shape
    rb = ROW_BLOCK
    tn = EXPERT_TN

    def used(b, nu):
        return jnp.minimum(b, nu[0] - 1)

    grid_spec = pltpu.PrefetchScalarGridSpec(
        num_scalar_prefetch=2,
        grid=(d // tn, n_blocks),
        in_specs=[pl.BlockSpec((rb, f), lambda n, b, be, nu: (used(b, nu), 0)),
                  pl.BlockSpec((None, f, tn), lambda n, b, be, nu: (be[used(b, nu)], 0, n)),
                  pl.BlockSpec((None, 1, tn), lambda n, b, be, nu: (be[used(b, nu)], 0, n))],
        out_specs=pl.BlockSpec((rb, tn), lambda n, b, be, nu: (b, n)),
        scratch_shapes=[pltpu.VMEM((f, tn), BF16)])
    return pl.pallas_call(
        _expert_down_kernel,
        grid_spec=grid_spec,
        out_shape=jax.ShapeDtypeStruct((n_blocks * rb, d), F32),
        compiler_params=_params(("arbitrary", "arbitrary"), 48),
        name="expert_down",
    )(block_e, n_used, act, w_down, b_down)


def _combine_kernel(dest_hbm, y_hbm, gate_ref, x1_ref, p_ref, g_ref, b_ref, wg_ref, bg_ref, wp_ref,
                    o_ref, idx_ref, rows_ref, idx_sem, sem, *, t):
    i = pl.program_id(0)
    idx_copy = pltpu.make_async_copy(dest_hbm.at[i], idx_ref, idx_sem)
    idx_copy.start()
    idx_copy.wait()

    def issue(r, carry):
        for k in range(TOP_K):
            pltpu.make_async_copy(y_hbm.at[pl.ds(idx_ref[r * TOP_K + k], 1)],
                                  rows_ref.at[k, pl.ds(r, 1)], sem).start()
        return carry

    lax.fori_loop(0, t, issue, 0, unroll=4)
    for k in range(TOP_K):
        pltpu.make_async_copy(y_hbm.at[pl.ds(0, t)], rows_ref.at[k], sem).wait()

    gates = gate_ref[...]
    f = functools.reduce(jnp.add, [gates[:, k:k + 1] * rows_ref[k] for k in range(TOP_K)])
    x2 = _layer_norm(DEEPNORM_ALPHA * x1_ref[...] + f, g_ref[...], b_ref[...])
    gate = jax.nn.sigmoid(_dot(x2.astype(BF16), wg_ref[...]) + bg_ref[...])
    emb = _dot(p_ref[...].astype(BF16), wp_ref[...])
    o_ref[...] = x2 + gate * emb


def _combine(dest, y_sorted, gates, x1, p2d, ln_g, ln_b, w_gate, b_gate, w_ple):
    n_tok, d = x1.shape
    ple = p2d.shape[1]
    t = TOKEN_TILE
    const = lambda i: (0, 0)
    tok = lambda i: (i, 0)
    return pl.pallas_call(
        functools.partial(_combine_kernel, t=t),
        grid=(n_tok // t,),
        in_specs=[pl.BlockSpec(memory_space=pl.ANY),
                  pl.BlockSpec(memory_space=pl.ANY),
                  pl.BlockSpec((t, TOP_K), tok),
                  pl.BlockSpec((t, d), tok),
                  pl.BlockSpec((t, ple), tok),
                  pl.BlockSpec((1, d), const),
                  pl.BlockSpec((1, d), const),
                  pl.BlockSpec((d, d), const),
                  pl.BlockSpec((1, d), const),
                  pl.BlockSpec((ple, d), const)],
        out_specs=pl.BlockSpec((t, d), tok),
        out_shape=jax.ShapeDtypeStruct((n_tok, d), F32),
        scratch_shapes=[pltpu.SMEM((t * TOP_K,), jnp.int32),
                        pltpu.VMEM((TOP_K, t, d), F32),
                        pltpu.SemaphoreType.DMA(()),
                        pltpu.SemaphoreType.DMA(())],
        compiler_params=_params(("arbitrary",), 48),
        name="combine_ln_ple",
    )(dest, y_sorted, gates, x1, p2d, ln_g, ln_b, w_gate, b_gate, w_ple)


def _routing_tables(top_e, rank, counts, n_blocks):
    n_tok = top_e.shape[0]
    rb = ROW_BLOCK
    counts = counts.reshape(-1).astype(jnp.int32)
    padded = (counts + rb - 1) // rb * rb
    pad_end = jnp.cumsum(padded)
    pad_start = pad_end - padded
    dest = pad_start[top_e] + rank
    tok = jnp.broadcast_to(jnp.arange(n_tok, dtype=jnp.int32)[:, None], dest.shape)
    row_tok = jnp.zeros((n_blocks * rb,), jnp.int32).at[dest.reshape(-1)].set(tok.reshape(-1))
    block_e = jnp.minimum(
        jnp.searchsorted(pad_end, jnp.arange(n_blocks, dtype=jnp.int32) * rb, side='right'),
        counts.shape[0] - 1).astype(jnp.int32)
    n_used = (pad_end[-1] // rb).astype(jnp.int32).reshape(1)
    return dest, row_tok.reshape(n_blocks, rb), block_e, n_used


def _const_tables(seq):
    t = ATT_TILE
    r = np.arange(t)
    u = (r[:, None] > r[None, :]).astype(np.float32)
    dist = (r[None, :] - r[:, None]).astype(np.float32)
    n_blk = seq // MOBA_BLOCK
    avg = np.zeros((n_blk, seq), np.float32)
    for n in range(n_blk):
        avg[n, n * MOBA_BLOCK:(n + 1) * MOBA_BLOCK] = 1.0 / MOBA_BLOCK
    tt = np.arange(TOKEN_TILE)
    tri = (tt[None, :] < tt[:, None]).astype(np.float32)
    sel = np.zeros((MXU_DIM, MXU_DIM // 2), np.float32)
    sel[2 * np.arange(MXU_DIM // 2), np.arange(MXU_DIM // 2)] = 1.0
    slopes = (2.0 ** (-8.0 * np.arange(1, N_HEADS_MOBA + 1) / N_HEADS_MOBA)).astype(np.float32)
    return (jnp.asarray(u, BF16), jnp.asarray(dist, F32), jnp.asarray(avg, BF16),
            jnp.asarray(tri, BF16), jnp.asarray(sel, BF16), jnp.asarray(slopes, F32))


def kernel(x, p, w_in, norm_sb, norm_moba, w_out, ln1_g, ln1_b, w_router, b_router, w_up, b_up,
           w_down, b_down, ln2_g, ln2_b, w_ple, w_ple_gate, b_ple_gate):
    batch, seq, d = x.shape
    n_tok = batch * seq
    sb_width = N_HEADS_SB * HEAD_DIM
    mb_width = N_HEADS_MOBA * HEAD_DIM
    n_exp = w_router.shape[-1]
    n_blocks = -(-n_tok * TOP_K // ROW_BLOCK) + n_exp
    u, dist, avg, tri, sel, slopes = _const_tables(seq)
    heads = sb_width // HEAD_DIM
    x2d = x.reshape(n_tok, d)

    for i in range(DEPTH):
        qkv = _qkv_proj(x2d, w_in[i].astype(BF16), q_tiles=(0, 3 * sb_width // PROJ_TN),
                        scale=1.0 / math.sqrt(HEAD_DIM)).reshape(batch, seq, -1)
        mixed_sb = _sb_attention(qkv, u, norm_sb[i].reshape(1, -1),
                                 q_col=0, k_col=heads, v_col=2 * heads, n_heads=N_HEADS_SB)
        mixed_mb = _moba_attention(qkv, slopes, avg, dist, norm_moba[i].reshape(1, -1),
                                   q_col=3 * heads, k_col=3 * heads + N_HEADS_MOBA,
                                   v_col=3 * heads + 2 * N_HEADS_MOBA, n_heads=N_HEADS_MOBA)
        x1, top_e, gates, rank, counts = _oproj_router(
            mixed_sb.reshape(n_tok, sb_width), mixed_mb.reshape(n_tok, mb_width), x2d,
            w_out[i].astype(BF16), ln1_g[i].reshape(1, -1), ln1_b[i].reshape(1, -1),
            w_router[i], b_router[i].reshape(1, -1), tri)
        dest, row_tok, block_e, n_used = _routing_tables(top_e, rank, counts, n_blocks)
        xs = _sort_gather(row_tok, x1, n_blocks)
        act = _expert_up(block_e, n_used, xs, w_up[i], b_up[i].reshape(n_exp, 1, -1), sel, n_blocks)
        y_sorted = _expert_down(block_e, n_used, act, w_down[i], b_down[i].reshape(n_exp, 1, -1),
                                n_blocks)
        x2d = _combine(dest.reshape(n_tok // TOKEN_TILE, TOKEN_TILE * TOP_K), y_sorted, gates, x1,
                       p[i].reshape(n_tok, -1), ln2_g[i].reshape(1, -1), ln2_b[i].reshape(1, -1),
                       w_ple_gate[i].astype(BF16), b_ple_gate[i].reshape(1, -1),
                       w_ple[i].astype(BF16))
    return x2d.reshape(batch, seq, d)
```

```python
import functools
import math

import jax
import jax.numpy as jnp
import numpy as np
from jax import lax
from jax.experimental import pallas as pl
from jax.experimental.pallas import tpu as pltpu

HEAD_DIM = 128
N_HEADS_SB = 8
N_HEADS_MOBA = 8
MOBA_BLOCK = 256
MOBA_TOPK = 3
N_EXPERTS = 32
TOP_K = 4
SWIGLU_LIMIT = 7.0
SWIGLU_ALPHA = 1.702
LN_EPS = 1e-5
RMS_EPS = 1e-6
NEG_INF = -1e30
DEPTH = 1
DEEPNORM_ALPHA = (2.0 * DEPTH) ** 0.25

ATT_TILE = 256
ATT_GROUP = 4
ROW_BLOCK = 256
TOKEN_TILE = 256
PROJ_TM = 1024
PROJ_TN = 1024
EXPERT_TN = 1024
MXU_DIM = 256
MIB = 1024 * 1024

F32 = jnp.float32
BF16 = jnp.bfloat16


def _params(semantics, vmem_mib):
    return pltpu.CompilerParams(dimension_semantics=semantics,
                                vmem_limit_bytes=vmem_mib * MIB)


def _split_bf16(a):
    hi = a.astype(BF16)
    lo = (a - hi.astype(F32)).astype(BF16)
    return hi, lo


def _dot(a, b):
    return jnp.dot(a, b, preferred_element_type=F32)


def _dot_nt(a, b):
    return lax.dot_general(a, b, (((1,), (1,)), ((), ())), preferred_element_type=F32)


def _qkv_kernel(x_ref, w_ref, o_ref, xb_ref, *, q_tiles, scale):
    n = pl.program_id(1)

    @pl.when(n == 0)
    def _():
        xb_ref[...] = x_ref[...].astype(BF16)

    acc = _dot(xb_ref[...], w_ref[...])
    is_q = functools.reduce(jnp.logical_or, [n == t for t in q_tiles])
    o_ref[...] = (acc * jnp.where(is_q, scale, 1.0)).astype(o_ref.dtype)


def _qkv_proj(x2d, w_bf16, q_tiles, scale):
    n_tok, d = x2d.shape
    n_out = w_bf16.shape[1]
    tm = min(PROJ_TM, n_tok)
    tn = PROJ_TN
    return pl.pallas_call(
        functools.partial(_qkv_kernel, q_tiles=q_tiles, scale=scale),
        grid=(n_tok // tm, n_out // tn),
        in_specs=[pl.BlockSpec((tm, d), lambda m, n: (m, 0)),
                  pl.BlockSpec((d, tn), lambda m, n: (0, n))],
        out_specs=pl.BlockSpec((tm, tn), lambda m, n: (m, n)),
        out_shape=jax.ShapeDtypeStruct((n_tok, n_out), BF16),
        scratch_shapes=[pltpu.VMEM((tm, d), BF16)],
        compiler_params=_params(("arbitrary", "arbitrary"), 48),
        name="qkv_proj",
    )(x2d, w_bf16)


def _head_rms(o, gain):
    ms = jnp.mean(o * o, axis=-1, keepdims=True)
    return o * lax.rsqrt(ms + RMS_EPS) * gain


def _sb_kernel(q_ref, k_ref, v_ref, u_ref, g_ref, o_ref, *, t, group):
    qi = pl.program_id(2)
    q = q_ref[...]
    u = u_ref[...]
    row = lax.broadcasted_iota(jnp.int32, (t, t), 0)
    col = lax.broadcasted_iota(jnp.int32, (t, t), 1)
    past = col < row

    def local(j, diag):
        k = k_ref[pl.ds(pl.multiple_of(j * t, t), t), :]
        z = _dot_nt(q, k)
        e = jnp.exp(-jnp.abs(z))
        log_keep = -(jnp.maximum(z, 0.0) + jnp.log(1.0 + e))
        if diag:
            log_keep = jnp.where(past, log_keep, 0.0)
        hi, lo = _split_bf16(log_keep)
        between = _dot(hi, u) + _dot(lo, u)
        total = between[:, :1] + log_keep[:, :1]
        return z + log_keep + between, total

    def finish(j, log_w, c, acc, diag, live=None):
        v = v_ref[pl.ds(pl.multiple_of(j * t, t), t), :]
        if live is not None:
            v = jnp.where(live, v, jnp.zeros_like(v))
        w = jnp.exp(log_w + c)
        if diag:
            w = jnp.where(past, w, 0.0)
        return acc + _dot(w.astype(BF16), v)

    log_w, total = local(qi, True)
    acc = finish(qi, log_w, jnp.zeros((t, 1), F32), jnp.zeros((t, HEAD_DIM), F32), True)

    def body(it, carry):
        c, acc = carry
        parts = []
        for g in range(group):
            j = qi - 1 - (it * group + g)
            live = None if g == 0 else j >= 0
            j = j if g == 0 else jnp.maximum(j, 0)
            parts.append((j, live) + local(j, False))
        for j, live, log_w, total in parts:
            acc = finish(j, log_w, c, acc, False, live)
            c = c + total
        return c, acc

    _, acc = lax.fori_loop(0, (qi + group - 1) // group, body, (total, acc))
    o_ref[...] = _head_rms(acc, g_ref[...]).astype(o_ref.dtype)


def _sb_attention(qkv, u, gain, *, q_col, k_col, v_col, n_heads):
    b, s, _ = qkv.shape
    t = ATT_TILE
    return pl.pallas_call(
        functools.partial(_sb_kernel, t=t, group=ATT_GROUP),
        grid=(b, n_heads, s // t),
        in_specs=[pl.BlockSpec((None, t, HEAD_DIM), lambda b, h, i: (b, i, q_col + h)),
                  pl.BlockSpec((None, s, HEAD_DIM), lambda b, h, i: (b, 0, k_col + h)),
                  pl.BlockSpec((None, s, HEAD_DIM), lambda b, h, i: (b, 0, v_col + h)),
                  pl.BlockSpec((t, t), lambda b, h, i: (0, 0)),
                  pl.BlockSpec((1, HEAD_DIM), lambda b, h, i: (0, h))],
        out_specs=pl.BlockSpec((None, t, HEAD_DIM), lambda b, h, i: (b, i, h)),
        out_shape=jax.ShapeDtypeStruct((b, s, n_heads * HEAD_DIM), BF16),
        compiler_params=_params(("arbitrary", "arbitrary", "arbitrary"), 32),
        name="sb_attention",
    )(qkv, qkv, qkv, u, gain)


def _moba_kernel(slope_ref, q_ref, k_ref, v_ref, avg_ref, dist_ref, g_ref, o_ref, km_ref,
                 *, t, n_blk, group):
    h = pl.program_id(1)
    qi = pl.program_id(2)

    @pl.when(qi == 0)
    def _():
        km_ref[...] = _dot(avg_ref[...], k_ref[...])

    q = q_ref[...]
    slope = slope_ref[h]
    alibi = slope * dist_ref[...]
    row = lax.broadcasted_iota(jnp.int32, (t, t), 0)
    col = lax.broadcasted_iota(jnp.int32, (t, t), 1)

    km_hi, km_lo = _split_bf16(km_ref[...])
    gate = _dot_nt(q, km_hi) + _dot_nt(q, km_lo)
    blk = lax.broadcasted_iota(jnp.int32, (t, n_blk), 1)
    valid = blk < qi
    work = jnp.where(valid, gate, -jnp.inf)
    sel = jnp.zeros((t, n_blk), jnp.bool_)
    for _ in range(MOBA_TOPK):
        m = jnp.max(work, axis=1, keepdims=True)
        idx = jnp.min(jnp.where(work == m, blk, n_blk), axis=1, keepdims=True)
        hit = blk == idx
        sel = jnp.logical_or(sel, hit)
        work = jnp.where(hit, -jnp.inf, work)
    sel_bias = jnp.where(jnp.logical_and(sel, valid), 0.0, NEG_INF)

    start = pl.multiple_of(qi * t, t)
    k = k_ref[pl.ds(start, t), :]
    v = v_ref[pl.ds(start, t), :]
    s = jnp.where(col <= row, _dot_nt(q, k) + alibi, NEG_INF)
    m0 = jnp.max(s, axis=1, keepdims=True)
    p = jnp.exp(s - m0)
    l0 = jnp.sum(p, axis=1, keepdims=True)
    acc0 = _dot(p.astype(BF16), v)

    def body(it, carry):
        m_run, l_run, acc = carry
        scores, values = [], []
        for g in range(group):
            j = it * group + g
            start = pl.multiple_of(j * t, t)
            col_bias = jnp.max(jnp.where(blk == j, sel_bias, -jnp.inf), axis=1, keepdims=True)
            off = slope * ((qi - j) * t).astype(F32)
            scores.append(_dot_nt(q, k_ref[pl.ds(start, t), :]) + alibi + (col_bias - off))
            values.append(v_ref[pl.ds(start, t), :])
        m_new = functools.reduce(
            jnp.maximum, [m_run] + [jnp.max(s, axis=1, keepdims=True) for s in scores])
        a = jnp.exp(m_run - m_new)
        l_run = a * l_run
        acc = a * acc
        for s, v in zip(scores, values):
            p = jnp.exp(s - m_new)
            l_run = l_run + jnp.sum(p, axis=1, keepdims=True)
            acc = acc + _dot(p.astype(BF16), v)
        return m_new, l_run, acc

    _, l_run, acc = lax.fori_loop(0, (qi + group - 1) // group, body, (m0, l0, acc0))
    o_ref[...] = _head_rms(acc / l_run, g_ref[...]).astype(o_ref.dtype)


def _moba_attention(qkv, slopes, avg, dist, gain, *, q_col, k_col, v_col, n_heads):
    b, s, _ = qkv.shape
    t = ATT_TILE
    n_blk = s // t
    assert s % t == 0 and n_blk % ATT_GROUP == 0
    grid_spec = pltpu.PrefetchScalarGridSpec(
        num_scalar_prefetch=1,
        grid=(b, n_heads, n_blk),
        in_specs=[pl.BlockSpec((None, t, HEAD_DIM), lambda b, h, i, sl: (b, i, q_col + h)),
                  pl.BlockSpec((None, s, HEAD_DIM), lambda b, h, i, sl: (b, 0, k_col + h)),
                  pl.BlockSpec((None, s, HEAD_DIM), lambda b, h, i, sl: (b, 0, v_col + h)),
                  pl.BlockSpec((n_blk, s), lambda b, h, i, sl: (0, 0)),
                  pl.BlockSpec((t, t), lambda b, h, i, sl: (0, 0)),
                  pl.BlockSpec((1, HEAD_DIM), lambda b, h, i, sl: (0, h))],
        out_specs=pl.BlockSpec((None, t, HEAD_DIM), lambda b, h, i, sl: (b, i, h)),
        scratch_shapes=[pltpu.VMEM((n_blk, HEAD_DIM), F32)])
    return pl.pallas_call(
        functools.partial(_moba_kernel, t=t, n_blk=n_blk, group=ATT_GROUP),
        grid_spec=grid_spec,
        out_shape=jax.ShapeDtypeStruct((b, s, n_heads * HEAD_DIM), BF16),
        compiler_params=_params(("arbitrary", "arbitrary", "arbitrary"), 32),
        name="moba_attention",
    )(slopes, qkv, qkv, qkv, avg, dist, gain)


def _layer_norm(x, g, b):
    mu = jnp.mean(x, axis=-1, keepdims=True)
    xc = x - mu
    var = jnp.mean(xc * xc, axis=-1, keepdims=True)
    return xc * lax.rsqrt(var + LN_EPS) * g + b


def _oproj_router_kernel(ms_ref, mm_ref, x_ref, wo_ref, g_ref, b_ref, wr_ref, br_ref, tri_ref,
                         x1_ref, te_ref, gt_ref, rk_ref, cnt_ref, carry_ref, *, sb_width):
    i = pl.program_id(0)

    @pl.when(i == 0)
    def _():
        carry_ref[...] = jnp.zeros_like(carry_ref)

    y = _dot(ms_ref[...], wo_ref[:sb_width, :]) + _dot(mm_ref[...], wo_ref[sb_width:, :])
    x1 = _layer_norm(DEEPNORM_ALPHA * x_ref[...] + y, g_ref[...], b_ref[...])
    x1_ref[...] = x1

    x_hi, x_lo = _split_bf16(x1)
    w_hi, w_lo = _split_bf16(wr_ref[...])
    logits = _dot(x_hi, w_hi) + _dot(x_hi, w_lo) + _dot(x_lo, w_hi) + br_ref[...]

    n_tok, n_exp = logits.shape
    lane = lax.broadcasted_iota(jnp.int32, (n_tok, n_exp), 1)
    work = logits
    vals, hits = [], []
    for _ in range(TOP_K):
        m = jnp.max(work, axis=1, keepdims=True)
        idx = jnp.min(jnp.where(work == m, lane, n_exp), axis=1, keepdims=True)
        hit = lane == idx
        vals.append(m)
        hits.append(hit)
        work = jnp.where(hit, -jnp.inf, work)
    exps = [jnp.exp(v - vals[0]) for v in vals]
    denom = functools.reduce(jnp.add, exps)

    onehot = functools.reduce(jnp.add, [hit.astype(F32) for hit in hits])
    rank_excl = carry_ref[...] + _dot(tri_ref[...], onehot.astype(BF16))
    carry_ref[...] = carry_ref[...] + jnp.sum(onehot, axis=0, keepdims=True)
    cnt_ref[...] = carry_ref[...]

    te_ref[...] = jnp.concatenate(
        [jnp.sum(jnp.where(hit, lane, 0), axis=1, keepdims=True) for hit in hits], axis=1)
    gt_ref[...] = jnp.concatenate([e / denom for e in exps], axis=1)
    rk_ref[...] = jnp.concatenate(
        [jnp.sum(jnp.where(hit, rank_excl, 0.0), axis=1, keepdims=True) for hit in hits],
        axis=1).astype(jnp.int32)


def _oproj_router(mixed_sb, mixed_mb, x2d, w_out, ln_g, ln_b, w_router, b_router, tri):
    n_tok, d = x2d.shape
    sb_width = mixed_sb.shape[1]
    mb_width = mixed_mb.shape[1]
    n_exp = w_router.shape[1]
    t = TOKEN_TILE
    const = lambda i: (0, 0)
    tok = lambda i: (i, 0)
    return pl.pallas_call(
        functools.partial(_oproj_router_kernel, sb_width=sb_width),
        grid=(n_tok // t,),
        in_specs=[pl.BlockSpec((t, sb_width), tok),
                  pl.BlockSpec((t, mb_width), tok),
                  pl.BlockSpec((t, d), tok),
                  pl.BlockSpec((sb_width + mb_width, d), const),
                  pl.BlockSpec((1, d), const),
                  pl.BlockSpec((1, d), const),
                  pl.BlockSpec((d, n_exp), const),
                  pl.BlockSpec((1, n_exp), const),
                  pl.BlockSpec((t, t), const)],
        out_specs=[pl.BlockSpec((t, d), tok),
                   pl.BlockSpec((t, TOP_K), tok),
                   pl.BlockSpec((t, TOP_K), tok),
                   pl.BlockSpec((t, TOP_K), tok),
                   pl.BlockSpec((1, n_exp), const)],
        out_shape=[jax.ShapeDtypeStruct((n_tok, d), F32),
                   jax.ShapeDtypeStruct((n_tok, TOP_K), jnp.int32),
                   jax.ShapeDtypeStruct((n_tok, TOP_K), F32),
                   jax.ShapeDtypeStruct((n_tok, TOP_K), jnp.int32),
                   jax.ShapeDtypeStruct((1, n_exp), F32)],
        scratch_shapes=[pltpu.VMEM((1, n_exp), F32)],
        compiler_params=_params(("arbitrary",), 48),
        name="oproj_ln_router",
    )(mixed_sb, mixed_mb, x2d, w_out, ln_g, ln_b, w_router, b_router, tri)


def _sort_gather_kernel(tok_hbm, x_hbm, o_ref, idx_ref, rows_ref, idx_sem, sems, *, rb):
    b = pl.program_id(0)
    slot = lax.rem(b, 2)

    def issue_block(blk, dst_slot):
        idx_copy = pltpu.make_async_copy(tok_hbm.at[blk], idx_ref, idx_sem)
        idx_copy.start()
        idx_copy.wait()

        def issue(r, carry):
            pltpu.make_async_copy(x_hbm.at[pl.ds(idx_ref[r], 1)],
                                  rows_ref.at[dst_slot, pl.ds(r, 1)], sems.at[dst_slot]).start()
            return carry

        lax.fori_loop(0, rb, issue, 0, unroll=8)

    @pl.when(b == 0)
    def _():
        issue_block(0, 0)

    @pl.when(b + 1 < pl.num_programs(0))
    def _():
        issue_block(b + 1, 1 - slot)

    pltpu.make_async_copy(x_hbm.at[pl.ds(0, rb)], rows_ref.at[slot], sems.at[slot]).wait()
    o_ref[...] = rows_ref[slot].astype(o_ref.dtype)


def _sort_gather(row_tok, x1, n_blocks):
    n_tok, d = x1.shape
    rb = ROW_BLOCK
    return pl.pallas_call(
        functools.partial(_sort_gather_kernel, rb=rb),
        grid=(n_blocks,),
        in_specs=[pl.BlockSpec(memory_space=pl.ANY), pl.BlockSpec(memory_space=pl.ANY)],
        out_specs=pl.BlockSpec((rb, d), lambda b: (b, 0)),
        out_shape=jax.ShapeDtypeStruct((n_blocks * rb, d), BF16),
        scratch_shapes=[pltpu.SMEM((rb,), jnp.int32),
                        pltpu.VMEM((2, rb, d), x1.dtype),
                        pltpu.SemaphoreType.DMA(()),
                        pltpu.SemaphoreType.DMA((2,))],
        compiler_params=_params(("arbitrary",), 16),
        name="sort_gather",
    )(row_tok, x1)


def _expert_changed(be_ref, b):
    prev = be_ref[jnp.maximum(b - 1, 0)]
    return jnp.logical_or(b == 0, be_ref[b] != prev)


def _expert_up_kernel(be_ref, nused_ref, xs_ref, w_ref, bias_ref, sel_ref, o_ref, wb_ref):
    b = pl.program_id(1)

    @pl.when(b >= nused_ref[0])
    def _():
        o_ref[...] = jnp.zeros_like(o_ref)

    @pl.when(b < nused_ref[0])
    def _():
        @pl.when(_expert_changed(be_ref, b))
        def _():
            wb_ref[...] = w_ref[...].astype(BF16)

        hdn = _dot(xs_ref[...], wb_ref[...]) + bias_ref[...]
        tn = hdn.shape[1]
        nxt = pltpu.roll(hdn, tn - 1, 1)
        glu = jnp.minimum(hdn, SWIGLU_LIMIT)
        lin = jnp.clip(nxt, -SWIGLU_LIMIT, SWIGLU_LIMIT)
        act = (glu * jax.nn.sigmoid(SWIGLU_ALPHA * glu) * (lin + 1.0)).astype(BF16)
        sel = sel_ref[...]
        o_ref[...] = jnp.concatenate(
            [_dot(act[:, c:c + MXU_DIM], sel) for c in range(0, tn, MXU_DIM)], axis=1).astype(o_ref.dtype)


def _expert_up(block_e, n_used, xs, w_up, b_up, sel, n_blocks):
    n_rows, d = xs.shape
    n_exp, _, f2 = w_up.shape
    rb = ROW_BLOCK
    tn = EXPERT_TN

    def used(b, nu):
        return jnp.minimum(b, nu[0] - 1)

    grid_spec = pltpu.PrefetchScalarGridSpec(
        num_scalar_prefetch=2,
        grid=(f2 // tn, n_blocks),
        in_specs=[pl.BlockSpec((rb, d), lambda n, b, be, nu: (used(b, nu), 0)),
                  pl.BlockSpec((None, d, tn), lambda n, b, be, nu: (be[used(b, nu)], 0, n)),
                  pl.BlockSpec((None, 1, tn), lambda n, b, be, nu: (be[used(b, nu)], 0, n)),
                  pl.BlockSpec((MXU_DIM, MXU_DIM // 2), lambda n, b, be, nu: (0, 0))],
        out_specs=pl.BlockSpec((rb, tn // 2), lambda n, b, be, nu: (b, n)),
        scratch_shapes=[pltpu.VMEM((d, tn), BF16)])
    return pl.pallas_call(
        _expert_up_kernel,
        grid_spec=grid_spec,
        out_shape=jax.ShapeDtypeStruct((n_rows, f2 // 2), BF16),
        compiler_params=_params(("arbitrary", "arbitrary"), 48),
        name="expert_up",
    )(block_e, n_used, xs, w_up, b_up, sel)


def _expert_down_kernel(be_ref, nused_ref, a_ref, w_ref, bias_ref, o_ref, wb_ref):
    b = pl.program_id(1)

    @pl.when(b >= nused_ref[0])
    def _():
        o_ref[...] = jnp.zeros_like(o_ref)

    @pl.when(b < nused_ref[0])
    def _():
        @pl.when(_expert_changed(be_ref, b))
        def _():
            wb_ref[...] = w_ref[...].astype(BF16)

        o_ref[...] = _dot(a_ref[...], wb_ref[...]) + bias_ref[...]


def _expert_down(block_e, n_used, act, w_down, b_down, n_blocks):
    n_exp, f, d = w_down.shape
    rb = ROW_BLOCK
    tn = EXPERT_TN

    def used(b, nu):
        return jnp.minimum(b, nu[0] - 1)

    grid_spec = pltpu.PrefetchScalarGridSpec(
        num_scalar_prefetch=2,
        grid=(d // tn, n_blocks),
        in_specs=[pl.BlockSpec((rb, f), lambda n, b, be, nu: (used(b, nu), 0)),
                  pl.BlockSpec((None, f, tn), lambda n, b, be, nu: (be[used(b, nu)], 0, n)),
                  pl.BlockSpec((None, 1, tn), lambda n, b, be, nu: (be[used(b, nu)], 0, n))],
        out_specs=pl.BlockSpec((rb, tn), lambda n, b, be, nu: (b, n)),
        scratch_shapes=[pltpu.VMEM((f, tn), BF16)])
    return pl.pallas_call(
        _expert_down_kernel,
        grid_spec=grid_spec,
        out_shape=jax.ShapeDtypeStruct((n_blocks * rb, d), F32),
        compiler_params=_params(("arbitrary", "arbitrary"), 48),
        name="expert_down",
    )(block_e, n_used, act, w_down, b_down)


def _combine_kernel(dest_hbm, y_hbm, gate_ref, x1_ref, p_ref, g_ref, b_ref, wg_ref, bg_ref, wp_ref,
                    o_ref, idx_ref, rows_ref, idx_sem, sem, *, t):
    i = pl.program_id(0)
    idx_copy = pltpu.make_async_copy(dest_hbm.at[i], idx_ref, idx_sem)
    idx_copy.start()
    idx_copy.wait()

    def issue(r, carry):
        for k in range(TOP_K):
            pltpu.make_async_copy(y_hbm.at[pl.ds(idx_ref[r * TOP_K + k], 1)],
                                  rows_ref.at[k, pl.ds(r, 1)], sem).start()
        return carry

    lax.fori_loop(0, t, issue, 0, unroll=4)
    for k in range(TOP_K):
        pltpu.make_async_copy(y_hbm.at[pl.ds(0, t)], rows_ref.at[k], sem).wait()

    gates = gate_ref[...]
    f = functools.reduce(jnp.add, [gates[:, k:k + 1] * rows_ref[k] for k in range(TOP_K)])
    x2 = _layer_norm(DEEPNORM_ALPHA * x1_ref[...] + f, g_ref[...], b_ref[...])
    gate = jax.nn.sigmoid(_dot(x2.astype(BF16), wg_ref[...]) + bg_ref[...])
    emb = _dot(p_ref[...].astype(BF16), wp_ref[...])
    o_ref[...] = x2 + gate * emb


def _combine(dest, y_sorted, gates, x1, p2d, ln_g, ln_b, w_gate, b_gate, w_ple):
    n_tok, d = x1.shape
    ple = p2d.shape[1]
    t = TOKEN_TILE
    const = lambda i: (0, 0)
    tok = lambda i: (i, 0)
    return pl.pallas_call(
        functools.partial(_combine_kernel, t=t),
        grid=(n_tok // t,),
        in_specs=[pl.BlockSpec(memory_space=pl.ANY),
                  pl.BlockSpec(memory_space=pl.ANY),
                  pl.BlockSpec((t, TOP_K), tok),
                  pl.BlockSpec((t, d), tok),
                  pl.BlockSpec((t, ple), tok),
                  pl.BlockSpec((1, d), const),
                  pl.BlockSpec((1, d), const),
                  pl.BlockSpec((d, d), const),
                  pl.BlockSpec((1, d), const),
                  pl.BlockSpec((ple, d), const)],
        out_specs=pl.BlockSpec((t, d), tok),
        out_shape=jax.ShapeDtypeStruct((n_tok, d), F32),
        scratch_shapes=[pltpu.SMEM((t * TOP_K,), jnp.int32),
                        pltpu.VMEM((TOP_K, t, d), F32),
                        pltpu.SemaphoreType.DMA(()),
                        pltpu.SemaphoreType.DMA(())],
        compiler_params=_params(("arbitrary",), 48),
        name="combine_ln_ple",
    )(dest, y_sorted, gates, x1, p2d, ln_g, ln_b, w_gate, b_gate, w_ple)


def _routing_tables(top_e, rank, counts, n_blocks):
    n_tok = top_e.shape[0]
    rb = ROW_BLOCK
    counts = counts.reshape(-1).astype(jnp.int32)
    padded = (counts + rb - 1) // rb * rb
    pad_end = jnp.cumsum(padded)
    pad_start = pad_end - padded
    dest = pad_start[top_e] + rank
    tok = jnp.broadcast_to(jnp.arange(n_tok, dtype=jnp.int32)[:, None], dest.shape)
    row_tok = jnp.zeros((n_blocks * rb,), jnp.int32).at[dest.reshape(-1)].set(tok.reshape(-1))
    block_start = jnp.arange(n_blocks, dtype=jnp.int32) * rb
    block_e = jnp.minimum(
        jnp.sum((pad_end[None, :] <= block_start[:, None]).astype(jnp.int32), axis=1),
        counts.shape[0] - 1)
    n_used = (pad_end[-1] // rb).astype(jnp.int32).reshape(1)
    return dest, row_tok.reshape(n_blocks, rb), block_e, n_used


def _const_tables(seq):
    t = ATT_TILE
    r = np.arange(t)
    u = (r[:, None] > r[None, :]).astype(np.float32)
    dist = (r[None, :] - r[:, None]).astype(np.float32)
    n_blk = seq // MOBA_BLOCK
    avg = np.zeros((n_blk, seq), np.float32)
    for n in range(n_blk):
        avg[n, n * MOBA_BLOCK:(n + 1) * MOBA_BLOCK] = 1.0 / MOBA_BLOCK
    tt = np.arange(TOKEN_TILE)
    tri = (tt[None, :] < tt[:, None]).astype(np.float32)
    sel = np.zeros((MXU_DIM, MXU_DIM // 2), np.float32)
    sel[2 * np.arange(MXU_DIM // 2), np.arange(MXU_DIM // 2)] = 1.0
    slopes = (2.0 ** (-8.0 * np.arange(1, N_HEADS_MOBA + 1) / N_HEADS_MOBA)).astype(np.float32)
    return (jnp.asarray(u, BF16), jnp.asarray(dist, F32), jnp.asarray(avg, BF16),
            jnp.asarray(tri, BF16), jnp.asarray(sel, BF16), jnp.asarray(slopes, F32))


def kernel(x, p, w_in, norm_sb, norm_moba, w_out, ln1_g, ln1_b, w_router, b_router, w_up, b_up,
           w_down, b_down, ln2_g, ln2_b, w_ple, w_ple_gate, b_ple_gate):
    batch, seq, d = x.shape
    n_tok = batch * seq
    sb_width = N_HEADS_SB * HEAD_DIM
    mb_width = N_HEADS_MOBA * HEAD_DIM
    n_exp = w_router.shape[-1]
    n_blocks = -(-n_tok * TOP_K // ROW_BLOCK) + n_exp
    u, dist, avg, tri, sel, slopes = _const_tables(seq)
    heads = sb_width // HEAD_DIM
    x2d = x.reshape(n_tok, d)

    for i in range(DEPTH):
        qkv = _qkv_proj(x2d, w_in[i].astype(BF16), q_tiles=(0, 3 * sb_width // PROJ_TN),
                        scale=1.0 / math.sqrt(HEAD_DIM)).reshape(batch, seq, -1)
        mixed_sb = _sb_attention(qkv, u, norm_sb[i].reshape(1, -1),
                                 q_col=0, k_col=heads, v_col=2 * heads, n_heads=N_HEADS_SB)
        mixed_mb = _moba_attention(qkv, slopes, avg, dist, norm_moba[i].reshape(1, -1),
                                   q_col=3 * heads, k_col=3 * heads + N_HEADS_MOBA,
                                   v_col=3 * heads + 2 * N_HEADS_MOBA, n_heads=N_HEADS_MOBA)
        x1, top_e, gates, rank, counts = _oproj_router(
            mixed_sb.reshape(n_tok, sb_width), mixed_mb.reshape(n_tok, mb_width), x2d,
            w_out[i].astype(BF16), ln1_g[i].reshape(1, -1), ln1_b[i].reshape(1, -1),
            w_router[i], b_router[i].reshape(1, -1), tri)
        dest, row_tok, block_e, n_used = _routing_tables(top_e, rank, counts, n_blocks)
        xs = _sort_gather(row_tok, x1, n_blocks)
        act = _expert_up(block_e, n_used, xs, w_up[i], b_up[i].reshape(n_exp, 1, -1), sel, n_blocks)
        y_sorted = _expert_down(block_e, n_used, act, w_down[i], b_down[i].reshape(n_exp, 1, -1),
                                n_blocks)
        x2d = _combine(dest.reshape(n_tok // TOKEN_TILE, TOKEN_TILE * TOP_K), y_sorted, gates, x1,
                       p[i].reshape(n_tok, -1), ln2_g[i].reshape(1, -1), ln2_b[i].reshape(1, -1),
                       w_ple_gate[i].astype(BF16), b_ple_gate[i].reshape(1, -1),
                       w_ple[i].astype(BF16))
    return x2d.reshape(batch, seq, d)
```

```python
import functools
import math

import jax
import jax.numpy as jnp
import numpy as np
from jax import lax
from jax.experimental import pallas as pl
from jax.experimental.pallas import tpu as pltpu

HEAD_DIM = 128
N_HEADS_SB = 8
N_HEADS_MOBA = 8
MOBA_BLOCK = 256
MOBA_TOPK = 3
N_EXPERTS = 32
TOP_K = 4
SWIGLU_LIMIT = 7.0
SWIGLU_ALPHA = 1.702
LN_EPS = 1e-5
RMS_EPS = 1e-6
NEG_INF = -1e30
DEPTH = 1
DEEPNORM_ALPHA = (2.0 * DEPTH) ** 0.25
LOG2_E = math.log2(math.e)

ATT_TILE = 256
ATT_GROUP = 4
ROW_BLOCK = 256
TOKEN_TILE = 256
PROJ_TM = 1024
PROJ_TN = 1024
EXPERT_TN = 1024
MXU_DIM = 256
DMA_UNROLL = 8
MIB = 1024 * 1024

F32 = jnp.float32
BF16 = jnp.bfloat16


def _params(semantics, vmem_mib):
    return pltpu.CompilerParams(dimension_semantics=semantics,
                                vmem_limit_bytes=vmem_mib * MIB)


def _split_bf16(a):
    hi = a.astype(BF16)
    lo = (a - hi.astype(F32)).astype(BF16)
    return hi, lo


def _dot(a, b):
    return jnp.dot(a, b, preferred_element_type=F32)


def _dot_nt(a, b):
    return lax.dot_general(a, b, (((1,), (1,)), ((), ())), preferred_element_type=F32)


def _qkv_kernel(x_ref, w_ref, o_ref, xb_ref, *, tile_scales):
    n = pl.program_id(1)

    @pl.when(n == 0)
    def _():
        xb_ref[...] = x_ref[...].astype(BF16)

    scale = jnp.float32(1.0)
    for tile, s in tile_scales:
        scale = jnp.where(n == tile, s, scale)
    o_ref[...] = (_dot(xb_ref[...], w_ref[...]) * scale).astype(o_ref.dtype)


def _qkv_proj(x2d, w_bf16, tile_scales):
    n_tok, d = x2d.shape
    n_out = w_bf16.shape[1]
    tm = min(PROJ_TM, n_tok)
    tn = PROJ_TN
    return pl.pallas_call(
        functools.partial(_qkv_kernel, tile_scales=tile_scales),
        grid=(n_tok // tm, n_out // tn),
        in_specs=[pl.BlockSpec((tm, d), lambda m, n: (m, 0)),
                  pl.BlockSpec((d, tn), lambda m, n: (0, n))],
        out_specs=pl.BlockSpec((tm, tn), lambda m, n: (m, n)),
        out_shape=jax.ShapeDtypeStruct((n_tok, n_out), BF16),
        scratch_shapes=[pltpu.VMEM((tm, d), BF16)],
        compiler_params=_params(("arbitrary", "arbitrary"), 48),
        name="qkv_proj",
    )(x2d, w_bf16)


def _head_rms(o, gain):
    ms = jnp.mean(o * o, axis=-1, keepdims=True)
    return o * lax.rsqrt(ms + RMS_EPS) * gain


def _head_rms_t(o_t, gain):
    ms = jnp.mean(o_t * o_t, axis=0, keepdims=True)
    return (o_t * lax.rsqrt(ms + RMS_EPS)).T * gain


def _store_transposed_values(v_ref, vt_ref, t):
    def body(n, carry):
        v = v_ref[pl.ds(pl.multiple_of(n * t, t), t), :]
        vt_ref[n] = v.astype(F32).T.astype(vt_ref.dtype)
        return carry

    lax.fori_loop(0, vt_ref.shape[0], body, 0)


def _neg_abs(x):
    bits = lax.bitcast_convert_type(x, jnp.uint32) | jnp.uint32(0x80000000)
    return lax.bitcast_convert_type(bits, F32)


def _sb_kernel(q_ref, k_ref, v_ref, u_ref, g_ref, o_ref, *, t, group):
    qi = pl.program_id(2)
    q = q_ref[...]
    u = u_ref[...]
    qry = lax.broadcasted_iota(jnp.int32, (t, t), 0)
    key = lax.broadcasted_iota(jnp.int32, (t, t), 1)
    past = key < qry

    def local(j, diag):
        k = k_ref[pl.ds(pl.multiple_of(j * t, t), t), :]
        z = _dot_nt(q, k)
        drop = jnp.maximum(z, 0.0) + jnp.log(1.0 + jnp.exp(_neg_abs(z)))
        if diag:
            drop = jnp.where(past, drop, 0.0)
        hi, lo = _split_bf16(drop)
        between = _dot(hi, u) + _dot(lo, u)
        first = slice(0, HEAD_DIM)
        total = (between[:, first] + drop[:, first])[:, :1]
        return z, drop + between, total

    def finish(j, z, spent, c, acc, diag, live=None):
        v = v_ref[pl.ds(pl.multiple_of(j * t, t), t), :]
        if live is not None:
            v = jnp.where(live, v, jnp.zeros_like(v))
        w = jnp.exp(z - (spent + c))
        if diag:
            w = jnp.where(past, w, 0.0)
        return acc + _dot(w.astype(BF16), v)

    z, spent, total = local(qi, True)
    acc = finish(qi, z, spent, jnp.zeros((t, 1), F32), jnp.zeros((t, HEAD_DIM), F32), True)

    def body(it, carry):
        c, acc = carry
        parts = []
        for g in range(group):
            j = qi - 1 - (it * group + g)
            live = None if g == 0 else j >= 0
            j = j if g == 0 else jnp.maximum(j, 0)
            parts.append((j, live) + local(j, False))
        for j, live, z, spent, total in parts:
            acc = finish(j, z, spent, c, acc, False, live)
            c = c + total
        return c, acc

    _, acc = lax.fori_loop(0, (qi + group - 1) // group, body, (total, acc))
    o_ref[...] = _head_rms(acc, g_ref[...]).astype(o_ref.dtype)


def _sb_attention(qkv, u, gain, *, q_col, k_col, v_col, n_heads):
    b, s, _ = qkv.shape
    t = ATT_TILE
    assert s % t == 0
    return pl.pallas_call(
        functools.partial(_sb_kernel, t=t, group=ATT_GROUP),
        grid=(b, n_heads, s // t),
        in_specs=[pl.BlockSpec((None, t, HEAD_DIM), lambda b, h, i: (b, i, q_col + h)),
                  pl.BlockSpec((None, s, HEAD_DIM), lambda b, h, i: (b, 0, k_col + h)),
                  pl.BlockSpec((None, s, HEAD_DIM), lambda b, h, i: (b, 0, v_col + h)),
                  pl.BlockSpec((t, t), lambda b, h, i: (0, 0)),
                  pl.BlockSpec((1, HEAD_DIM), lambda b, h, i: (0, h))],
        out_specs=pl.BlockSpec((None, t, HEAD_DIM), lambda b, h, i: (b, i, h)),
        out_shape=jax.ShapeDtypeStruct((b, s, n_heads * HEAD_DIM), BF16),
        compiler_params=_params(("arbitrary", "arbitrary", "arbitrary"), 32),
        name="sb_attention",
    )(qkv, qkv, qkv, u, gain)


def _moba_kernel(slope_ref, q_ref, k_ref, v_ref, avg_ref, dist_ref, g_ref, o_ref,
                 km_ref, vt_ref, bias_ref, sc_ref, *, t, n_blk, group):
    h = pl.program_id(1)
    qi = pl.program_id(2)

    @pl.when(qi == 0)
    def _():
        km_ref[...] = _dot(avg_ref[...], k_ref[...])
        _store_transposed_values(v_ref, vt_ref, t)

    q = q_ref[...]
    slope = slope_ref[h]
    alibi = slope * dist_ref[...]
    key = lax.broadcasted_iota(jnp.int32, (t, t), 0)
    qry = lax.broadcasted_iota(jnp.int32, (t, t), 1)

    km_hi, km_lo = _split_bf16(km_ref[...])
    gate = _dot_nt(km_hi, q) + _dot_nt(km_lo, q)
    blk = lax.broadcasted_iota(jnp.int32, (n_blk, t), 0)
    valid = blk < qi
    work = jnp.where(valid, gate, -jnp.inf)
    sel = jnp.zeros((n_blk, t), jnp.bool_)
    for _ in range(MOBA_TOPK):
        m = jnp.max(work, axis=0, keepdims=True)
        idx = jnp.min(jnp.where(work == m, blk, n_blk), axis=0, keepdims=True)
        hit = blk == idx
        sel = jnp.logical_or(sel, hit)
        work = jnp.where(hit, -jnp.inf, work)
    bias_ref[...] = jnp.where(jnp.logical_and(sel, valid), 0.0, NEG_INF)

    k = k_ref[pl.ds(pl.multiple_of(qi * t, t), t), :]
    s = jnp.where(key <= qry, _dot_nt(k, q) + alibi, NEG_INF)
    m0 = jnp.max(s, axis=0, keepdims=True)
    p = jnp.exp2(s - m0)
    l0 = jnp.sum(p, axis=0, keepdims=True)
    acc0 = _dot(vt_ref[qi], p.astype(BF16))

    def store_scores(grp, slot):
        for g in range(group):
            j = grp * group + g
            k = k_ref[pl.ds(pl.multiple_of(j * t, t), t), :]
            shift = bias_ref[pl.ds(j, 1), :] - slope * ((qi - j) * t).astype(F32)
            sc_ref[slot, g] = _dot_nt(k, q) + alibi + shift

    def fold(grp, slot, carry):
        m_run, l_run, acc = carry
        scores = [sc_ref[slot, g] for g in range(group)]
        m_new = functools.reduce(
            jnp.maximum, [m_run] + [jnp.max(s, axis=0, keepdims=True) for s in scores])
        a = jnp.exp2(m_run - m_new)
        l_run = a * l_run
        acc = a * acc
        for g, s in enumerate(scores):
            p = jnp.exp2(s - m_new)
            l_run = l_run + jnp.sum(p, axis=0, keepdims=True)
            acc = acc + _dot(vt_ref[grp * group + g], p.astype(BF16))
        return m_new, l_run, acc

    def body(it, carry):
        carry = fold(it, lax.rem(it, 2), carry)
        store_scores(it + 1, lax.rem(it + 1, 2))
        return carry

    last = jnp.maximum((qi + group - 1) // group - 1, 0)
    store_scores(0, 0)
    carry = lax.fori_loop(0, last, body, (m0, l0, acc0))
    _, l_run, acc = fold(last, lax.rem(last, 2), carry)
    o_ref[...] = _head_rms_t(acc / l_run, g_ref[...]).astype(o_ref.dtype)


def _moba_attention(qkv, slopes, avg, dist_t, gain, *, q_col, k_col, v_col, n_heads):
    b, s, _ = qkv.shape
    t = ATT_TILE
    n_blk = s // t
    assert s % t == 0 and n_blk % ATT_GROUP == 0
    grid_spec = pltpu.PrefetchScalarGridSpec(
        num_scalar_prefetch=1,
        grid=(b, n_heads, n_blk),
        in_specs=[pl.BlockSpec((None, t, HEAD_DIM), lambda b, h, i, sl: (b, i, q_col + h)),
                  pl.BlockSpec((None, s, HEAD_DIM), lambda b, h, i, sl: (b, 0, k_col + h)),
                  pl.BlockSpec((None, s, HEAD_DIM), lambda b, h, i, sl: (b, 0, v_col + h)),
                  pl.BlockSpec((n_blk, s), lambda b, h, i, sl: (0, 0)),
                  pl.BlockSpec((t, t), lambda b, h, i, sl: (0, 0)),
                  pl.BlockSpec((1, HEAD_DIM), lambda b, h, i, sl: (0, h))],
        out_specs=pl.BlockSpec((None, t, HEAD_DIM), lambda b, h, i, sl: (b, i, h)),
        scratch_shapes=[pltpu.VMEM((n_blk, HEAD_DIM), F32),
                        pltpu.VMEM((n_blk, HEAD_DIM, t), BF16),
                        pltpu.VMEM((n_blk, t), F32),
                        pltpu.VMEM((2, ATT_GROUP, t, t), F32)])
    return pl.pallas_call(
        functools.partial(_moba_kernel, t=t, n_blk=n_blk, group=ATT_GROUP),
        grid_spec=grid_spec,
        out_shape=jax.ShapeDtypeStruct((b, s, n_heads * HEAD_DIM), BF16),
        compiler_params=_params(("arbitrary", "arbitrary", "arbitrary"), 32),
        name="moba_attention",
    )(slopes, qkv, qkv, qkv, avg, dist_t, gain)


def _layer_norm(x, g, b):
    mu = jnp.mean(x, axis=-1, keepdims=True)
    xc = x - mu
    var = jnp.mean(xc * xc, axis=-1, keepdims=True)
    return xc * lax.rsqrt(var + LN_EPS) * g + b


def _oproj_router_kernel(ms_ref, mm_ref, x_ref, wo_ref, g_ref, b_ref, wr_ref, br_ref, tri_ref,
                         x1_ref, te_ref, gt_ref, rk_ref, cnt_ref, carry_ref, *, sb_width):
    i = pl.program_id(0)

    @pl.when(i == 0)
    def _():
        carry_ref[...] = jnp.zeros_like(carry_ref)

    y = _dot(ms_ref[...], wo_ref[:sb_width, :]) + _dot(mm_ref[...], wo_ref[sb_width:, :])
    x1 = _layer_norm(DEEPNORM_ALPHA * x_ref[...] + y, g_ref[...], b_ref[...])
    x1_ref[...] = x1

    x_hi, x_lo = _split_bf16(x1)
    w_hi, w_lo = _split_bf16(wr_ref[...])
    logits = _dot(x_hi, w_hi) + _dot(x_hi, w_lo) + _dot(x_lo, w_hi) + br_ref[...]

    n_tok, n_exp = logits.shape
    lane = lax.broadcasted_iota(jnp.int32, (n_tok, n_exp), 1)
    work = logits
    vals, hits = [], []
    for _ in range(TOP_K):
        m = jnp.max(work, axis=1, keepdims=True)
        idx = jnp.min(jnp.where(work == m, lane, n_exp), axis=1, keepdims=True)
        hit = lane == idx
        vals.append(m)
        hits.append(hit)
        work = jnp.where(hit, -jnp.inf, work)
    exps = [jnp.exp(v - vals[0]) for v in vals]
    denom = functools.reduce(jnp.add, exps)

    onehot = functools.reduce(jnp.add, [hit.astype(F32) for hit in hits])
    rank_excl = carry_ref[...] + _dot(tri_ref[...], onehot.astype(BF16))
    carry_ref[...] = carry_ref[...] + jnp.sum(onehot, axis=0, keepdims=True)
    cnt_ref[...] = carry_ref[...]

    te_ref[...] = jnp.concatenate(
        [jnp.sum(jnp.where(hit, lane, 0), axis=1, keepdims=True) for hit in hits], axis=1)
    gt_ref[...] = jnp.concatenate([e / denom for e in exps], axis=1)
    rk_ref[...] = jnp.concatenate(
        [jnp.sum(jnp.where(hit, rank_excl, 0.0), axis=1, keepdims=True) for hit in hits],
        axis=1).astype(jnp.int32)


def _oproj_router(mixed_sb, mixed_mb, x2d, w_out, ln_g, ln_b, w_router, b_router, tri):
    n_tok, d = x2d.shape
    sb_width = mixed_sb.shape[1]
    mb_width = mixed_mb.shape[1]
    n_exp = w_router.shape[1]
    t = TOKEN_TILE
    const = lambda i: (0, 0)
    tok = lambda i: (i, 0)
    return pl.pallas_call(
        functools.partial(_oproj_router_kernel, sb_width=sb_width),
        grid=(n_tok // t,),
        in_specs=[pl.BlockSpec((t, sb_width), tok),
                  pl.BlockSpec((t, mb_width), tok),
                  pl.BlockSpec((t, d), tok),
                  pl.BlockSpec((sb_width + mb_width, d), const),
                  pl.BlockSpec((1, d), const),
                  pl.BlockSpec((1, d), const),
                  pl.BlockSpec((d, n_exp), const),
                  pl.BlockSpec((1, n_exp), const),
                  pl.BlockSpec((t, t), const)],
        out_specs=[pl.BlockSpec((t, d), tok),
                   pl.BlockSpec((t, TOP_K), tok),
                   pl.BlockSpec((t, TOP_K), tok),
                   pl.BlockSpec((t, TOP_K), tok),
                   pl.BlockSpec((1, n_exp), const)],
        out_shape=[jax.ShapeDtypeStruct((n_tok, d), F32),
                   jax.ShapeDtypeStruct((n_tok, TOP_K), jnp.int32),
                   jax.ShapeDtypeStruct((n_tok, TOP_K), F32),
                   jax.ShapeDtypeStruct((n_tok, TOP_K), jnp.int32),
                   jax.ShapeDtypeStruct((1, n_exp), F32)],
        scratch_shapes=[pltpu.VMEM((1, n_exp), F32)],
        compiler_params=_params(("arbitrary",), 48),
        name="oproj_ln_router",
    )(mixed_sb, mixed_mb, x2d, w_out, ln_g, ln_b, w_router, b_router, tri)


def _issue_row_copies(n_rows, make_copy):
    def body(it, carry):
        for u in range(DMA_UNROLL):
            make_copy(it * DMA_UNROLL + u).start(priority=u % 2)
        return carry

    lax.fori_loop(0, n_rows // DMA_UNROLL, body, 0)


def _sort_gather_kernel(tok_hbm, x_hbm, o_ref, idx_ref, rows_ref, idx_sem, sems, *, rb):
    b = pl.program_id(0)
    slot = lax.rem(b, 2)

    def issue_block(blk, dst_slot):
        idx_copy = pltpu.make_async_copy(tok_hbm.at[blk], idx_ref, idx_sem)
        idx_copy.start()
        idx_copy.wait()
        _issue_row_copies(rb, lambda r: pltpu.make_async_copy(
            x_hbm.at[pl.ds(idx_ref[r], 1)], rows_ref.at[dst_slot, pl.ds(r, 1)], sems.at[dst_slot]))

    @pl.when(b == 0)
    def _():
        issue_block(0, 0)

    @pl.when(b + 1 < pl.num_programs(0))
    def _():
        issue_block(b + 1, 1 - slot)

    pltpu.make_async_copy(x_hbm.at[pl.ds(0, rb)], rows_ref.at[slot], sems.at[slot]).wait()
    o_ref[...] = rows_ref[slot].astype(o_ref.dtype)


def _sort_gather(row_tok, x1, n_blocks):
    n_tok, d = x1.shape
    rb = ROW_BLOCK
    return pl.pallas_call(
        functools.partial(_sort_gather_kernel, rb=rb),
        grid=(n_blocks,),
        in_specs=[pl.BlockSpec(memory_space=pl.ANY), pl.BlockSpec(memory_space=pl.ANY)],
        out_specs=pl.BlockSpec((rb, d), lambda b: (b, 0)),
        out_shape=jax.ShapeDtypeStruct((n_blocks * rb, d), BF16),
        scratch_shapes=[pltpu.SMEM((rb,), jnp.int32),
                        pltpu.VMEM((2, rb, d), x1.dtype),
                        pltpu.SemaphoreType.DMA(()),
                        pltpu.SemaphoreType.DMA((2,))],
        compiler_params=_params(("arbitrary",), 16),
        name="sort_gather",
    )(row_tok, x1)


def _expert_changed(be_ref, b):
    prev = be_ref[jnp.maximum(b - 1, 0)]
    return jnp.logical_or(b == 0, be_ref[b] != prev)


def _expert_up_kernel(be_ref, nused_ref, xs_ref, w_ref, bias_ref, sel_ref, o_ref, wb_ref):
    b = pl.program_id(1)

    @pl.when(b >= nused_ref[0])
    def _():
        o_ref[...] = jnp.zeros_like(o_ref)

    @pl.when(b < nused_ref[0])
    def _():
        @pl.when(_expert_changed(be_ref, b))
        def _():
            wb_ref[...] = w_ref[...].astype(BF16)

        hdn = _dot(xs_ref[...], wb_ref[...]) + bias_ref[...]
        tn = hdn.shape[1]
        nxt = pltpu.roll(hdn, tn - 1, 1)
        glu = jnp.minimum(hdn, SWIGLU_LIMIT)
        lin = jnp.clip(nxt, -SWIGLU_LIMIT, SWIGLU_LIMIT)
        act = (glu * jax.nn.sigmoid(SWIGLU_ALPHA * glu) * (lin + 1.0)).astype(BF16)
        sel = sel_ref[...]
        o_ref[...] = jnp.concatenate(
            [_dot(act[:, c:c + MXU_DIM], sel) for c in range(0, tn, MXU_DIM)], axis=1).astype(o_ref.dtype)


def _expert_up(block_e, n_used, xs, w_up, b_up, sel, n_blocks):
    n_rows, d = xs.shape
    n_exp, _, f2 = w_up.shape
    rb = ROW_BLOCK
    tn = EXPERT_TN

    def used(b, nu):
        return jnp.minimum(b, nu[0] - 1)

    grid_spec = pltpu.PrefetchScalarGridSpec(
        num_scalar_prefetch=2,
        grid=(f2 // tn, n_blocks),
        in_specs=[pl.BlockSpec((rb, d), lambda n, b, be, nu: (used(b, nu), 0)),
                  pl.BlockSpec((None, d, tn), lambda n, b, be, nu: (be[used(b, nu)], 0, n)),
                  pl.BlockSpec((None, 1, tn), lambda n, b, be, nu: (be[used(b, nu)], 0, n)),
                  pl.BlockSpec((MXU_DIM, MXU_DIM // 2), lambda n, b, be, nu: (0, 0))],
        out_specs=pl.BlockSpec((rb, tn // 2), lambda n, b, be, nu: (b, n)),
        scratch_shapes=[pltpu.VMEM((d, tn), BF16)])
    return pl.pallas_call(
        _expert_up_kernel,
        grid_spec=grid_spec,
        out_shape=jax.ShapeDtypeStruct((n_rows, f2 // 2), BF16),
        compiler_params=_params(("arbitrary", "arbitrary"), 48),
        name="expert_up",
    )(block_e, n_used, xs, w_up, b_up, sel)


def _expert_down_kernel(be_ref, nused_ref, a_ref, w_ref, bias_ref, o_ref, wb_ref):
    b = pl.program_id(1)

    @pl.when(b >= nused_ref[0])
    def _():
        o_ref[...] = jnp.zeros_like(o_ref)

    @pl.when(b < nused_ref[0])
    def _():
        @pl.when(_expert_changed(be_ref, b))
        def _():
            wb_ref[...] = w_ref[...].astype(BF16)

        o_ref[...] = _dot(a_ref[...], wb_ref[...]) + bias_ref[...]


def _expert_down(block_e, n_used, act, w_down, b_down, n_blocks):
    n_exp, f, d = w_down.shape
    rb = ROW_BLOCK
    tn = EXPERT_TN

    def used(b, nu):
        return jnp.minimum(b, nu[0] - 1)

    grid_spec = pltpu.PrefetchScalarGridSpec(
        num_scalar_prefetch=2,
        grid=(d // tn, n_blocks),
        in_specs=[pl.BlockSpec((rb, f), lambda n, b, be, nu: (used(b, nu), 0)),
                  pl.BlockSpec((None, f, tn), lambda n, b, be, nu: (be[used(b, nu)], 0, n)),
                  pl.BlockSpec((None, 1, tn), lambda n, b, be, nu: (be[used(b, nu)], 0, n))],
        out_specs=pl.BlockSpec((rb, tn), lambda n, b, be, nu: (b, n)),
        scratch_shapes=[pltpu.VMEM((f, tn), BF16)])
    return pl.pallas_call(
        _expert_down_kernel,
        grid_spec=grid_spec,
        out_shape=jax.ShapeDtypeStruct((n_blocks * rb, d), F32),
        compiler_params=_params(("arbitrary", "arbitrary"), 48),
        name="expert_down",
    )(block_e, n_used, act, w_down, b_down)


def _combine_kernel(dest_hbm, y_hbm, gate_ref, x1_ref, p_ref, g_ref, b_ref, wg_ref, bg_ref, wp_ref,
                    o_ref, idx_ref, rows_ref, idx_sem, sem, *, t):
    i = pl.program_id(0)
    idx_copy = pltpu.make_async_copy(dest_hbm.at[i], idx_ref, idx_sem)
    idx_copy.start()
    idx_copy.wait()
    _issue_row_copies(TOP_K * t, lambda n: pltpu.make_async_copy(
        y_hbm.at[pl.ds(idx_ref[n], 1)], rows_ref.at[pl.ds(n, 1)], sem))
    pltpu.make_async_copy(y_hbm.at[pl.ds(0, TOP_K * t)], rows_ref, sem).wait()

    gates = gate_ref[...]
    f = functools.reduce(
        jnp.add, [gates[:, k:k + 1] * rows_ref[pl.ds(k * t, t), :] for k in range(TOP_K)])
    x2 = _layer_norm(DEEPNORM_ALPHA * x1_ref[...] + f, g_ref[...], b_ref[...])
    gate = jax.nn.sigmoid(_dot(x2.astype(BF16), wg_ref[...]) + bg_ref[...])
    emb = _dot(p_ref[...].astype(BF16), wp_ref[...])
    o_ref[...] = x2 + gate * emb


def _combine(dest, y_sorted, gates, x1, p2d, ln_g, ln_b, w_gate, b_gate, w_ple):
    n_tok, d = x1.shape
    ple = p2d.shape[1]
    t = TOKEN_TILE
    const = lambda i: (0, 0)
    tok = lambda i: (i, 0)
    return pl.pallas_call(
        functools.partial(_combine_kernel, t=t),
        grid=(n_tok // t,),
        in_specs=[pl.BlockSpec(memory_space=pl.ANY),
                  pl.BlockSpec(memory_space=pl.ANY),
                  pl.BlockSpec((t, TOP_K), tok),
                  pl.BlockSpec((t, d), tok),
                  pl.BlockSpec((t, ple), tok),
                  pl.BlockSpec((1, d), const),
                  pl.BlockSpec((1, d), const),
                  pl.BlockSpec((d, d), const),
                  pl.BlockSpec((1, d), const),
                  pl.BlockSpec((ple, d), const)],
        out_specs=pl.BlockSpec((t, d), tok),
        out_shape=jax.ShapeDtypeStruct((n_tok, d), F32),
        scratch_shapes=[pltpu.SMEM((t * TOP_K,), jnp.int32),
                        pltpu.VMEM((TOP_K * t, d), F32),
                        pltpu.SemaphoreType.DMA(()),
                        pltpu.SemaphoreType.DMA(())],
        compiler_params=_params(("arbitrary",), 48),
        name="combine_ln_ple",
    )(dest, y_sorted, gates, x1, p2d, ln_g, ln_b, w_gate, b_gate, w_ple)


def _routing_tables(top_e, rank, counts, n_blocks):
    n_tok = top_e.shape[0]
    rb = ROW_BLOCK
    counts = counts.reshape(-1).astype(jnp.int32)
    padded = (counts + rb - 1) // rb * rb
    pad_end = jnp.cumsum(padded)
    pad_start = pad_end - padded
    dest = pad_start[top_e] + rank
    tok = jnp.broadcast_to(jnp.arange(n_tok, dtype=jnp.int32)[:, None], dest.shape)
    row_tok = jnp.zeros((n_blocks * rb,), jnp.int32).at[dest.reshape(-1)].set(tok.reshape(-1))
    block_start = jnp.arange(n_blocks, dtype=jnp.int32) * rb
    block_e = jnp.minimum(
        jnp.sum((pad_end[None, :] <= block_start[:, None]).astype(jnp.int32), axis=1),
        counts.shape[0] - 1)
    n_used = (pad_end[-1] // rb).astype(jnp.int32).reshape(1)
    dest_tiles = dest.reshape(n_tok // TOKEN_TILE, TOKEN_TILE, TOP_K).transpose(0, 2, 1)
    return (dest, dest_tiles.reshape(n_tok // TOKEN_TILE, TOP_K * TOKEN_TILE),
            row_tok.reshape(n_blocks, rb), block_e, n_used)


def _const_tables(seq):
    t = ATT_TILE
    r = np.arange(t)
    u = (r[:, None] > r[None, :]).astype(np.float32)
    dist_t = (r[:, None] - r[None, :]).astype(np.float32)
    n_blk = seq // MOBA_BLOCK
    avg = np.zeros((n_blk, seq), np.float32)
    for n in range(n_blk):
        avg[n, n * MOBA_BLOCK:(n + 1) * MOBA_BLOCK] = 1.0 / MOBA_BLOCK
    tt = np.arange(TOKEN_TILE)
    tri = (tt[None, :] < tt[:, None]).astype(np.float32)
    sel = np.zeros((MXU_DIM, MXU_DIM // 2), np.float32)
    sel[2 * np.arange(MXU_DIM // 2), np.arange(MXU_DIM // 2)] = 1.0
    slopes = 2.0 ** (-8.0 * np.arange(1, N_HEADS_MOBA + 1) / N_HEADS_MOBA)
    return (jnp.asarray(u, BF16), jnp.asarray(dist_t, F32), jnp.asarray(avg, BF16),
            jnp.asarray(tri, BF16), jnp.asarray(sel, BF16),
            jnp.asarray(slopes * LOG2_E, F32))


def kernel(x, p, w_in, norm_sb, norm_moba, w_out, ln1_g, ln1_b, w_router, b_router, w_up, b_up,
           w_down, b_down, ln2_g, ln2_b, w_ple, w_ple_gate, b_ple_gate):
    batch, seq, d = x.shape
    n_tok = batch * seq
    sb_width = N_HEADS_SB * HEAD_DIM
    mb_width = N_HEADS_MOBA * HEAD_DIM
    n_exp = w_router.shape[-1]
    n_blocks = -(-n_tok * TOP_K // ROW_BLOCK) + n_exp
    u, dist_t, avg, tri, sel, slopes2 = _const_tables(seq)
    heads = sb_width // HEAD_DIM
    x2d = x.reshape(n_tok, d)
    q_scales = ((0, 1.0 / math.sqrt(HEAD_DIM)),
                (3 * sb_width // PROJ_TN, LOG2_E / math.sqrt(HEAD_DIM)))

    for i in range(DEPTH):
        qkv = _qkv_proj(x2d, w_in[i].astype(BF16), q_scales).reshape(batch, seq, -1)
        mixed_sb = _sb_attention(qkv, u, norm_sb[i].reshape(1, -1),
                                 q_col=0, k_col=heads, v_col=2 * heads, n_heads=N_HEADS_SB)
        mixed_mb = _moba_attention(qkv, slopes2, avg, dist_t, norm_moba[i].reshape(1, -1),
                                   q_col=3 * heads, k_col=3 * heads + N_HEADS_MOBA,
                                   v_col=3 * heads + 2 * N_HEADS_MOBA, n_heads=N_HEADS_MOBA)
        x1, top_e, gates, rank, counts = _oproj_router(
            mixed_sb.reshape(n_tok, sb_width), mixed_mb.reshape(n_tok, mb_width), x2d,
            w_out[i].astype(BF16), ln1_g[i].reshape(1, -1), ln1_b[i].reshape(1, -1),
            w_router[i], b_router[i].reshape(1, -1), tri)
        _, dest_tiles, row_tok, block_e, n_used = _routing_tables(top_e, rank, counts, n_blocks)
        xs = _sort_gather(row_tok, x1, n_blocks)
        act = _expert_up(block_e, n_used, xs, w_up[i], b_up[i].reshape(n_exp, 1, -1), sel, n_blocks)
        y_sorted = _expert_down(block_e, n_used, act, w_down[i], b_down[i].reshape(n_exp, 1, -1),
                                n_blocks)
        x2d = _combine(dest_tiles, y_sorted, gates, x1,
                       p[i].reshape(n_tok, -1), ln2_g[i].reshape(1, -1), ln2_b[i].reshape(1, -1),
                       w_ple_gate[i].astype(BF16), b_ple_gate[i].reshape(1, -1),
                       w_ple[i].astype(BF16))
    return x2d.reshape(batch, seq, d)
```

```python
import functools
import math

import jax
import jax.numpy as jnp
import numpy as np
from jax import lax
from jax.experimental import pallas as pl
from jax.experimental.pallas import tpu as pltpu

HEAD_DIM = 128
N_HEADS_SB = 8
N_HEADS_MOBA = 8
MOBA_BLOCK = 256
MOBA_TOPK = 3
N_EXPERTS = 32
TOP_K = 4
SWIGLU_LIMIT = 7.0
SWIGLU_ALPHA = 1.702
LN_EPS = 1e-5
RMS_EPS = 1e-6
NEG_INF = -1e30
DEPTH = 1
DEEPNORM_ALPHA = (2.0 * DEPTH) ** 0.25
LOG2_E = math.log2(math.e)

ATT_TILE = 256
ATT_GROUP = 4
SB_QUERY_TILE = 512
SB_GROUP = 2
ROW_BLOCK = 512
TOKEN_TILE = 256
PROJ_TM = 1024
PROJ_TN = 1024
EXPERT_TN = 1024
MXU_DIM = 256
DMA_UNROLL = 8
MIB = 1024 * 1024

F32 = jnp.float32
BF16 = jnp.bfloat16


def _params(semantics, vmem_mib):
    return pltpu.CompilerParams(dimension_semantics=semantics,
                                vmem_limit_bytes=vmem_mib * MIB)


def _split_bf16(a):
    hi = a.astype(BF16)
    lo = (a - hi.astype(F32)).astype(BF16)
    return hi, lo


def _dot(a, b):
    return jnp.dot(a, b, preferred_element_type=F32)


def _dot_nt(a, b):
    return lax.dot_general(a, b, (((1,), (1,)), ((), ())), preferred_element_type=F32)


def _qkv_kernel(x_ref, w_ref, o_ref, xb_ref, *, tile_scales):
    n = pl.program_id(1)

    @pl.when(n == 0)
    def _():
        xb_ref[...] = x_ref[...].astype(BF16)

    scale = jnp.float32(1.0)
    for tile, s in tile_scales:
        scale = jnp.where(n == tile, s, scale)
    o_ref[...] = (_dot(xb_ref[...], w_ref[...]) * scale).astype(o_ref.dtype)


def _qkv_proj(x2d, w_bf16, tile_scales):
    n_tok, d = x2d.shape
    n_out = w_bf16.shape[1]
    tm = min(PROJ_TM, n_tok)
    tn = PROJ_TN
    return pl.pallas_call(
        functools.partial(_qkv_kernel, tile_scales=tile_scales),
        grid=(n_tok // tm, n_out // tn),
        in_specs=[pl.BlockSpec((tm, d), lambda m, n: (m, 0)),
                  pl.BlockSpec((d, tn), lambda m, n: (0, n))],
        out_specs=pl.BlockSpec((tm, tn), lambda m, n: (m, n)),
        out_shape=jax.ShapeDtypeStruct((n_tok, n_out), BF16),
        scratch_shapes=[pltpu.VMEM((tm, d), BF16)],
        compiler_params=_params(("arbitrary", "arbitrary"), 48),
        name="qkv_proj",
    )(x2d, w_bf16)


def _head_rms(o, gain):
    ms = jnp.mean(o * o, axis=-1, keepdims=True)
    return o * lax.rsqrt(ms + RMS_EPS) * gain


def _head_rms_t(o_t, gain):
    ms = jnp.mean(o_t * o_t, axis=0, keepdims=True)
    return (o_t * lax.rsqrt(ms + RMS_EPS)).T * gain


def _store_transposed_values(v_ref, vt_ref, t):
    def body(n, carry):
        v = v_ref[pl.ds(pl.multiple_of(n * t, t), t), :]
        vt_ref[n] = v.astype(F32).T.astype(vt_ref.dtype)
        return carry

    lax.fori_loop(0, vt_ref.shape[0], body, 0)


def _sb_kernel(q_ref, k_ref, v_ref, u_ref, g_ref, o_ref, *, tq, t, group):
    qi = pl.program_id(2)
    ratio = tq // t
    q = q_ref[...]
    u = u_ref[...]
    qry = lax.broadcasted_iota(jnp.int32, (tq, t), 0)
    key = lax.broadcasted_iota(jnp.int32, (tq, t), 1)

    def local(j, past):
        k = k_ref[pl.ds(pl.multiple_of(j * t, t), t), :]
        z = _dot_nt(q, k)
        drop = jnp.maximum(z, 0.0) + jnp.log(1.0 + jnp.exp(-jnp.abs(z)))
        if past is not None:
            drop = jnp.where(past, drop, 0.0)
        hi, lo = _split_bf16(drop)
        between = _dot(hi, u) + _dot(lo, u)
        first = slice(0, HEAD_DIM)
        total = (between[:, first] + drop[:, first])[:, :1]
        return z, drop + between, total

    def finish(j, z, spent, c, acc, past):
        v = v_ref[pl.ds(pl.multiple_of(j * t, t), t), :]
        w = jnp.exp(z - (spent + c))
        if past is not None:
            w = jnp.where(past, w, 0.0)
        return acc + _dot(w.astype(BF16), v)

    def sweep(tiles, c, acc):
        parts = [(j, past) + local(j, past) for j, past in tiles]
        for j, past, z, spent, total in parts:
            acc = finish(j, z, spent, c, acc, past)
            c = c + total
        return c, acc

    diagonal = [(qi * ratio + d, key + d * t < qry) for d in reversed(range(ratio))]
    carry = sweep(diagonal, jnp.zeros((tq, 1), F32), jnp.zeros((tq, HEAD_DIM), F32))

    def body(it, carry):
        first = qi * ratio - 1 - it * group
        return sweep([(first - g, None) for g in range(group)], *carry)

    _, acc = lax.fori_loop(0, qi * (ratio // group), body, carry)
    o_ref[...] = _head_rms(acc, g_ref[...]).astype(o_ref.dtype)


def _sb_attention(qkv, u, gain, *, q_col, k_col, v_col, n_heads):
    b, s, _ = qkv.shape
    t = ATT_TILE
    tq = SB_QUERY_TILE
    assert s % tq == 0 and tq % t == 0 and (tq // t) % SB_GROUP == 0
    return pl.pallas_call(
        functools.partial(_sb_kernel, tq=tq, t=t, group=SB_GROUP),
        grid=(b, n_heads, s // tq),
        in_specs=[pl.BlockSpec((None, tq, HEAD_DIM), lambda b, h, i: (b, i, q_col + h)),
                  pl.BlockSpec((None, s, HEAD_DIM), lambda b, h, i: (b, 0, k_col + h)),
                  pl.BlockSpec((None, s, HEAD_DIM), lambda b, h, i: (b, 0, v_col + h)),
                  pl.BlockSpec((t, t), lambda b, h, i: (0, 0)),
                  pl.BlockSpec((1, HEAD_DIM), lambda b, h, i: (0, h))],
        out_specs=pl.BlockSpec((None, tq, HEAD_DIM), lambda b, h, i: (b, i, h)),
        out_shape=jax.ShapeDtypeStruct((b, s, n_heads * HEAD_DIM), BF16),
        compiler_params=_params(("arbitrary", "arbitrary", "arbitrary"), 32),
        name="sb_attention",
    )(qkv, qkv, qkv, u, gain)


def _moba_kernel(slope_ref, q_ref, k_ref, v_ref, avg_ref, dist_ref, g_ref, o_ref,
                 km_ref, vt_ref, bias_ref, sc_ref, *, t, n_blk, group):
    h = pl.program_id(1)
    qi = pl.program_id(2)

    @pl.when(qi == 0)
    def _():
        km_ref[...] = _dot(avg_ref[...], k_ref[...])
        _store_transposed_values(v_ref, vt_ref, t)

    q = q_ref[...]
    slope = slope_ref[h]
    alibi = slope * dist_ref[...]
    key = lax.broadcasted_iota(jnp.int32, (t, t), 0)
    qry = lax.broadcasted_iota(jnp.int32, (t, t), 1)

    km_hi, km_lo = _split_bf16(km_ref[...])
    gate = _dot_nt(km_hi, q) + _dot_nt(km_lo, q)
    blk = lax.broadcasted_iota(jnp.int32, (n_blk, t), 0)
    valid = blk < qi
    work = jnp.where(valid, gate, -jnp.inf)
    sel = jnp.zeros((n_blk, t), jnp.bool_)
    for _ in range(MOBA_TOPK):
        m = jnp.max(work, axis=0, keepdims=True)
        idx = jnp.min(jnp.where(work == m, blk, n_blk), axis=0, keepdims=True)
        hit = blk == idx
        sel = jnp.logical_or(sel, hit)
        work = jnp.where(hit, -jnp.inf, work)
    bias_ref[...] = jnp.where(jnp.logical_and(sel, valid), 0.0, NEG_INF)

    k = k_ref[pl.ds(pl.multiple_of(qi * t, t), t), :]
    s = jnp.where(key <= qry, _dot_nt(k, q) + alibi, NEG_INF)
    m0 = jnp.max(s, axis=0, keepdims=True)
    p = jnp.exp2(s - m0)
    l0 = jnp.sum(p, axis=0, keepdims=True)
    acc0 = _dot(vt_ref[qi], p.astype(BF16))

    def store_scores(grp, slot):
        for g in range(group):
            j = grp * group + g
            k = k_ref[pl.ds(pl.multiple_of(j * t, t), t), :]
            shift = bias_ref[pl.ds(j, 1), :] - slope * ((qi - j) * t).astype(F32)
            sc_ref[slot, g] = _dot_nt(k, q) + alibi + shift

    def fold(grp, slot, carry):
        m_run, l_run, acc = carry
        scores = [sc_ref[slot, g] for g in range(group)]
        m_new = functools.reduce(
            jnp.maximum, [m_run] + [jnp.max(s, axis=0, keepdims=True) for s in scores])
        a = jnp.exp2(m_run - m_new)
        l_run = a * l_run
        acc = a * acc
        for g, s in enumerate(scores):
            p = jnp.exp2(s - m_new)
            l_run = l_run + jnp.sum(p, axis=0, keepdims=True)
            acc = acc + _dot(vt_ref[grp * group + g], p.astype(BF16))
        return m_new, l_run, acc

    def body(it, carry):
        carry = fold(it, lax.rem(it, 2), carry)
        store_scores(it + 1, lax.rem(it + 1, 2))
        return carry

    last = jnp.maximum((qi + group - 1) // group - 1, 0)
    store_scores(0, 0)
    carry = lax.fori_loop(0, last, body, (m0, l0, acc0))
    _, l_run, acc = fold(last, lax.rem(last, 2), carry)
    o_ref[...] = _head_rms_t(acc / l_run, g_ref[...]).astype(o_ref.dtype)


def _moba_attention(qkv, slopes, avg, dist_t, gain, *, q_col, k_col, v_col, n_heads):
    b, s, _ = qkv.shape
    t = ATT_TILE
    n_blk = s // t
    assert s % t == 0 and n_blk % ATT_GROUP == 0
    grid_spec = pltpu.PrefetchScalarGridSpec(
        num_scalar_prefetch=1,
        grid=(b, n_heads, n_blk),
        in_specs=[pl.BlockSpec((None, t, HEAD_DIM), lambda b, h, i, sl: (b, i, q_col + h)),
                  pl.BlockSpec((None, s, HEAD_DIM), lambda b, h, i, sl: (b, 0, k_col + h)),
                  pl.BlockSpec((None, s, HEAD_DIM), lambda b, h, i, sl: (b, 0, v_col + h)),
                  pl.BlockSpec((n_blk, s), lambda b, h, i, sl: (0, 0)),
                  pl.BlockSpec((t, t), lambda b, h, i, sl: (0, 0)),
                  pl.BlockSpec((1, HEAD_DIM), lambda b, h, i, sl: (0, h))],
        out_specs=pl.BlockSpec((None, t, HEAD_DIM), lambda b, h, i, sl: (b, i, h)),
        scratch_shapes=[pltpu.VMEM((n_blk, HEAD_DIM), F32),
                        pltpu.VMEM((n_blk, HEAD_DIM, t), BF16),
                        pltpu.VMEM((n_blk, t), F32),
                        pltpu.VMEM((2, ATT_GROUP, t, t), F32)])
    return pl.pallas_call(
        functools.partial(_moba_kernel, t=t, n_blk=n_blk, group=ATT_GROUP),
        grid_spec=grid_spec,
        out_shape=jax.ShapeDtypeStruct((b, s, n_heads * HEAD_DIM), BF16),
        compiler_params=_params(("arbitrary", "arbitrary", "arbitrary"), 32),
        name="moba_attention",
    )(slopes, qkv, qkv, qkv, avg, dist_t, gain)


def _layer_norm(x, g, b):
    mu = jnp.mean(x, axis=-1, keepdims=True)
    xc = x - mu
    var = jnp.mean(xc * xc, axis=-1, keepdims=True)
    return xc * lax.rsqrt(var + LN_EPS) * g + b


def _oproj_router_kernel(ms_ref, mm_ref, x_ref, wo_ref, g_ref, b_ref, wr_ref, br_ref, tri_ref,
                         x1_ref, te_ref, gt_ref, rk_ref, cnt_ref, carry_ref, *, sb_width):
    i = pl.program_id(0)

    @pl.when(i == 0)
    def _():
        carry_ref[...] = jnp.zeros_like(carry_ref)

    y = _dot(ms_ref[...], wo_ref[:sb_width, :]) + _dot(mm_ref[...], wo_ref[sb_width:, :])
    x1 = _layer_norm(DEEPNORM_ALPHA * x_ref[...] + y, g_ref[...], b_ref[...])
    x1_ref[...] = x1

    x_hi, x_lo = _split_bf16(x1)
    w_hi, w_lo = _split_bf16(wr_ref[...])
    logits = _dot(x_hi, w_hi) + _dot(x_hi, w_lo) + _dot(x_lo, w_hi) + br_ref[...]

    n_tok, n_exp = logits.shape
    lane = lax.broadcasted_iota(jnp.int32, (n_tok, n_exp), 1)
    work = logits
    vals, hits = [], []
    for _ in range(TOP_K):
        m = jnp.max(work, axis=1, keepdims=True)
        idx = jnp.min(jnp.where(work == m, lane, n_exp), axis=1, keepdims=True)
        hit = lane == idx
        vals.append(m)
        hits.append(hit)
        work = jnp.where(hit, -jnp.inf, work)
    exps = [jnp.exp(v - vals[0]) for v in vals]
    denom = functools.reduce(jnp.add, exps)

    onehot = functools.reduce(jnp.add, [hit.astype(F32) for hit in hits])
    rank_excl = carry_ref[...] + _dot(tri_ref[...], onehot.astype(BF16))
    carry_ref[...] = carry_ref[...] + jnp.sum(onehot, axis=0, keepdims=True)
    cnt_ref[...] = carry_ref[...]

    te_ref[...] = jnp.concatenate(
        [jnp.sum(jnp.where(hit, lane, 0), axis=1, keepdims=True) for hit in hits], axis=1)
    gt_ref[...] = jnp.concatenate([e / denom for e in exps], axis=1)
    rk_ref[...] = jnp.concatenate(
        [jnp.sum(jnp.where(hit, rank_excl, 0.0), axis=1, keepdims=True) for hit in hits],
        axis=1).astype(jnp.int32)


def _oproj_router(mixed_sb, mixed_mb, x2d, w_out, ln_g, ln_b, w_router, b_router, tri):
    n_tok, d = x2d.shape
    sb_width = mixed_sb.shape[1]
    mb_width = mixed_mb.shape[1]
    n_exp = w_router.shape[1]
    t = TOKEN_TILE
    const = lambda i: (0, 0)
    tok = lambda i: (i, 0)
    return pl.pallas_call(
        functools.partial(_oproj_router_kernel, sb_width=sb_width),
        grid=(n_tok // t,),
        in_specs=[pl.BlockSpec((t, sb_width), tok),
                  pl.BlockSpec((t, mb_width), tok),
                  pl.BlockSpec((t, d), tok),
                  pl.BlockSpec((sb_width + mb_width, d), const),
                  pl.BlockSpec((1, d), const),
                  pl.BlockSpec((1, d), const),
                  pl.BlockSpec((d, n_exp), const),
                  pl.BlockSpec((1, n_exp), const),
                  pl.BlockSpec((t, t), const)],
        out_specs=[pl.BlockSpec((t, d), tok),
                   pl.BlockSpec((t, TOP_K), tok),
                   pl.BlockSpec((t, TOP_K), tok),
                   pl.BlockSpec((t, TOP_K), tok),
                   pl.BlockSpec((1, n_exp), const)],
        out_shape=[jax.ShapeDtypeStruct((n_tok, d), F32),
                   jax.ShapeDtypeStruct((n_tok, TOP_K), jnp.int32),
                   jax.ShapeDtypeStruct((n_tok, TOP_K), F32),
                   jax.ShapeDtypeStruct((n_tok, TOP_K), jnp.int32),
                   jax.ShapeDtypeStruct((1, n_exp), F32)],
        scratch_shapes=[pltpu.VMEM((1, n_exp), F32)],
        compiler_params=_params(("arbitrary",), 48),
        name="oproj_ln_router",
    )(mixed_sb, mixed_mb, x2d, w_out, ln_g, ln_b, w_router, b_router, tri)


def _issue_row_copies(n_rows, make_copy):
    def body(it, carry):
        for u in range(DMA_UNROLL):
            make_copy(it * DMA_UNROLL + u).start(priority=u % 2)
        return carry

    lax.fori_loop(0, n_rows // DMA_UNROLL, body, 0)


def _gathered_rows(step, n_steps, idx_hbm, src_hbm, idx_ref, rows_ref, idx_sems, row_sems):
    n_rows = rows_ref.shape[1]

    def idx_copy(s):
        slot = lax.rem(s, 2)
        return pltpu.make_async_copy(idx_hbm.at[s], idx_ref.at[slot], idx_sems.at[slot])

    def issue(s):
        slot = lax.rem(s, 2)
        idx_copy(s).wait()
        _issue_row_copies(n_rows, lambda r: pltpu.make_async_copy(
            src_hbm.at[pl.ds(idx_ref[slot, r], 1)], rows_ref.at[slot, pl.ds(r, 1)],
            row_sems.at[slot]))

    @pl.when(step == 0)
    def _():
        idx_copy(0).start()
        issue(0)
        if n_steps > 1:
            idx_copy(1).start()

    @pl.when(step + 1 < n_steps)
    def _():
        issue(step + 1)

        @pl.when(step + 2 < n_steps)
        def _():
            idx_copy(step + 2).start()

    slot = lax.rem(step, 2)
    pltpu.make_async_copy(src_hbm.at[pl.ds(0, n_rows)], rows_ref.at[slot], row_sems.at[slot]).wait()
    return slot


def _gather_scratch(n_rows, width, dtype):
    return [pltpu.SMEM((2, n_rows), jnp.int32),
            pltpu.VMEM((2, n_rows, width), dtype),
            pltpu.SemaphoreType.DMA((2,)),
            pltpu.SemaphoreType.DMA((2,))]


def _sort_gather_kernel(tok_hbm, x_hbm, o_ref, idx_ref, rows_ref, idx_sems, row_sems, *, n_steps):
    slot = _gathered_rows(pl.program_id(0), n_steps, tok_hbm, x_hbm,
                          idx_ref, rows_ref, idx_sems, row_sems)
    o_ref[...] = rows_ref[slot].astype(o_ref.dtype)


def _sort_gather(row_tok, x1, n_blocks):
    n_tok, d = x1.shape
    rb = ROW_BLOCK
    return pl.pallas_call(
        functools.partial(_sort_gather_kernel, n_steps=n_blocks),
        grid=(n_blocks,),
        in_specs=[pl.BlockSpec(memory_space=pl.ANY), pl.BlockSpec(memory_space=pl.ANY)],
        out_specs=pl.BlockSpec((rb, d), lambda b: (b, 0)),
        out_shape=jax.ShapeDtypeStruct((n_blocks * rb, d), BF16),
        scratch_shapes=_gather_scratch(rb, d, x1.dtype),
        compiler_params=_params(("arbitrary",), 24),
        name="sort_gather",
    )(row_tok, x1)


def _expert_changed(be_ref, b):
    prev = be_ref[jnp.maximum(b - 1, 0)]
    return jnp.logical_or(b == 0, be_ref[b] != prev)


def _expert_up_kernel(be_ref, nused_ref, xs_ref, w_ref, bias_ref, sel_ref, o_ref, wb_ref):
    b = pl.program_id(1)

    @pl.when(b >= nused_ref[0])
    def _():
        o_ref[...] = jnp.zeros_like(o_ref)

    @pl.when(b < nused_ref[0])
    def _():
        @pl.when(_expert_changed(be_ref, b))
        def _():
            wb_ref[...] = w_ref[...].astype(BF16)

        hdn = _dot(xs_ref[...], wb_ref[...]) + bias_ref[...]
        tn = hdn.shape[1]
        nxt = pltpu.roll(hdn, tn - 1, 1)
        glu = jnp.minimum(hdn, SWIGLU_LIMIT)
        lin = jnp.clip(nxt, -SWIGLU_LIMIT, SWIGLU_LIMIT)
        act = (glu * jax.nn.sigmoid(SWIGLU_ALPHA * glu) * (lin + 1.0)).astype(BF16)
        sel = sel_ref[...]
        o_ref[...] = jnp.concatenate(
            [_dot(act[:, c:c + MXU_DIM], sel) for c in range(0, tn, MXU_DIM)], axis=1).astype(o_ref.dtype)


def _expert_up(block_e, n_used, xs, w_up, b_up, sel, n_blocks):
    n_rows, d = xs.shape
    n_exp, _, f2 = w_up.shape
    rb = ROW_BLOCK
    tn = EXPERT_TN

    def used(b, nu):
        return jnp.minimum(b, nu[0] - 1)

    grid_spec = pltpu.PrefetchScalarGridSpec(
        num_scalar_prefetch=2,
        grid=(f2 // tn, n_blocks),
        in_specs=[pl.BlockSpec((rb, d), lambda n, b, be, nu: (used(b, nu), 0)),
                  pl.BlockSpec((None, d, tn), lambda n, b, be, nu: (be[used(b, nu)], 0, n)),
                  pl.BlockSpec((None, 1, tn), lambda n, b, be, nu: (be[used(b, nu)], 0, n)),
                  pl.BlockSpec((MXU_DIM, MXU_DIM // 2), lambda n, b, be, nu: (0, 0))],
        out_specs=pl.BlockSpec((rb, tn // 2), lambda n, b, be, nu: (b, n)),
        scratch_shapes=[pltpu.VMEM((d, tn), BF16)])
    return pl.pallas_call(
        _expert_up_kernel,
        grid_spec=grid_spec,
        out_shape=jax.ShapeDtypeStruct((n_rows, f2 // 2), BF16),
        compiler_params=_params(("arbitrary", "arbitrary"), 48),
        name="expert_up",
    )(block_e, n_used, xs, w_up, b_up, sel)


def _expert_down_kernel(be_ref, nused_ref, a_ref, w_ref, bias_ref, o_ref, wb_ref):
    b = pl.program_id(1)

    @pl.when(b >= nused_ref[0])
    def _():
        o_ref[...] = jnp.zeros_like(o_ref)

    @pl.when(b < nused_ref[0])
    def _():
        @pl.when(_expert_changed(be_ref, b))
        def _():
            wb_ref[...] = w_ref[...].astype(BF16)

        o_ref[...] = _dot(a_ref[...], wb_ref[...]) + bias_ref[...]


def _expert_down(block_e, n_used, act, w_down, b_down, n_blocks):
    n_exp, f, d = w_down.shape
    rb = ROW_BLOCK
    tn = EXPERT_TN

    def used(b, nu):
        return jnp.minimum(b, nu[0] - 1)

    grid_spec = pltpu.PrefetchScalarGridSpec(
        num_scalar_prefetch=2,
        grid=(d // tn, n_blocks),
        in_specs=[pl.BlockSpec((rb, f), lambda n, b, be, nu: (used(b, nu), 0)),
                  pl.BlockSpec((None, f, tn), lambda n, b, be, nu: (be[used(b, nu)], 0, n)),
                  pl.BlockSpec((None, 1, tn), lambda n, b, be, nu: (be[used(b, nu)], 0, n))],
        out_specs=pl.BlockSpec((rb, tn), lambda n, b, be, nu: (b, n)),
        scratch_shapes=[pltpu.VMEM((f, tn), BF16)])
    return pl.pallas_call(
        _expert_down_kernel,
        grid_spec=grid_spec,
        out_shape=jax.ShapeDtypeStruct((n_blocks * rb, d), F32),
        compiler_params=_params(("arbitrary", "arbitrary"), 48),
        name="expert_down",
    )(block_e, n_used, act, w_down, b_down)


def _combine_kernel(dest_hbm, y_hbm, gate_ref, x1_ref, p_ref, g_ref, b_ref, wg_ref, bg_ref, wp_ref,
                    o_ref, idx_ref, rows_ref, idx_sems, row_sems, *, t, n_steps):
    slot = _gathered_rows(pl.program_id(0), n_steps, dest_hbm, y_hbm,
                          idx_ref, rows_ref, idx_sems, row_sems)
    gates = gate_ref[...]
    f = functools.reduce(
        jnp.add, [gates[:, k:k + 1] * rows_ref[slot, pl.ds(k * t, t), :] for k in range(TOP_K)])
    x2 = _layer_norm(DEEPNORM_ALPHA * x1_ref[...] + f, g_ref[...], b_ref[...])
    gate = jax.nn.sigmoid(_dot(x2.astype(BF16), wg_ref[...]) + bg_ref[...])
    emb = _dot(p_ref[...].astype(BF16), wp_ref[...])
    o_ref[...] = x2 + gate * emb


def _combine(dest, y_sorted, gates, x1, p2d, ln_g, ln_b, w_gate, b_gate, w_ple):
    n_tok, d = x1.shape
    ple = p2d.shape[1]
    t = TOKEN_TILE
    const = lambda i: (0, 0)
    tok = lambda i: (i, 0)
    return pl.pallas_call(
        functools.partial(_combine_kernel, t=t, n_steps=n_tok // t),
        grid=(n_tok // t,),
        in_specs=[pl.BlockSpec(memory_space=pl.ANY),
                  pl.BlockSpec(memory_space=pl.ANY),
                  pl.BlockSpec((t, TOP_K), tok),
                  pl.BlockSpec((t, d), tok),
                  pl.BlockSpec((t, ple), tok),
                  pl.BlockSpec((1, d), const),
                  pl.BlockSpec((1, d), const),
                  pl.BlockSpec((d, d), const),
                  pl.BlockSpec((1, d), const),
                  pl.BlockSpec((ple, d), const)],
        out_specs=pl.BlockSpec((t, d), tok),
        out_shape=jax.ShapeDtypeStruct((n_tok, d), F32),
        scratch_shapes=_gather_scratch(TOP_K * t, d, y_sorted.dtype),
        compiler_params=_params(("arbitrary",), 56),
        name="combine_ln_ple",
    )(dest, y_sorted, gates, x1, p2d, ln_g, ln_b, w_gate, b_gate, w_ple)


def _routing_tables(top_e, rank, counts, n_blocks):
    n_tok = top_e.shape[0]
    rb = ROW_BLOCK
    counts = counts.reshape(-1).astype(jnp.int32)
    padded = (counts + rb - 1) // rb * rb
    pad_end = jnp.cumsum(padded)
    pad_start = pad_end - padded
    dest = pad_start[top_e] + rank
    tok = jnp.broadcast_to(jnp.arange(n_tok, dtype=jnp.int32)[:, None], dest.shape)
    row_tok = jnp.zeros((n_blocks * rb,), jnp.int32).at[dest.reshape(-1)].set(tok.reshape(-1))
    block_start = jnp.arange(n_blocks, dtype=jnp.int32) * rb
    block_e = jnp.minimum(
        jnp.sum((pad_end[None, :] <= block_start[:, None]).astype(jnp.int32), axis=1),
        counts.shape[0] - 1)
    n_used = (pad_end[-1] // rb).astype(jnp.int32).reshape(1)
    dest_tiles = dest.reshape(n_tok // TOKEN_TILE, TOKEN_TILE, TOP_K).transpose(0, 2, 1)
    return (dest, dest_tiles.reshape(n_tok // TOKEN_TILE, TOP_K * TOKEN_TILE),
            row_tok.reshape(n_blocks, rb), block_e, n_used)


def _const_tables(seq):
    t = ATT_TILE
    r = np.arange(t)
    u = (r[:, None] > r[None, :]).astype(np.float32)
    dist_t = (r[:, None] - r[None, :]).astype(np.float32)
    n_blk = seq // MOBA_BLOCK
    avg = np.zeros((n_blk, seq), np.float32)
    for n in range(n_blk):
        avg[n, n * MOBA_BLOCK:(n + 1) * MOBA_BLOCK] = 1.0 / MOBA_BLOCK
    tt = np.arange(TOKEN_TILE)
    tri = (tt[None, :] < tt[:, None]).astype(np.float32)
    sel = np.zeros((MXU_DIM, MXU_DIM // 2), np.float32)
    sel[2 * np.arange(MXU_DIM // 2), np.arange(MXU_DIM // 2)] = 1.0
    slopes = 2.0 ** (-8.0 * np.arange(1, N_HEADS_MOBA + 1) / N_HEADS_MOBA)
    return (jnp.asarray(u, BF16), jnp.asarray(dist_t, F32), jnp.asarray(avg, BF16),
            jnp.asarray(tri, BF16), jnp.asarray(sel, BF16),
            jnp.asarray(slopes * LOG2_E, F32))


def kernel(x, p, w_in, norm_sb, norm_moba, w_out, ln1_g, ln1_b, w_router, b_router, w_up, b_up,
           w_down, b_down, ln2_g, ln2_b, w_ple, w_ple_gate, b_ple_gate):
    batch, seq, d = x.shape
    n_tok = batch * seq
    sb_width = N_HEADS_SB * HEAD_DIM
    mb_width = N_HEADS_MOBA * HEAD_DIM
    n_exp = w_router.shape[-1]
    n_blocks = -(-n_tok * TOP_K // ROW_BLOCK) + n_exp
    u, dist_t, avg, tri, sel, slopes2 = _const_tables(seq)
    heads = sb_width // HEAD_DIM
    x2d = x.reshape(n_tok, d)
    q_scales = ((0, 1.0 / math.sqrt(HEAD_DIM)),
                (3 * sb_width // PROJ_TN, LOG2_E / math.sqrt(HEAD_DIM)))

    for i in range(DEPTH):
        qkv = _qkv_proj(x2d, w_in[i].astype(BF16), q_scales).reshape(batch, seq, -1)
        mixed_sb = _sb_attention(qkv, u, norm_sb[i].reshape(1, -1),
                                 q_col=0, k_col=heads, v_col=2 * heads, n_heads=N_HEADS_SB)
        mixed_mb = _moba_attention(qkv, slopes2, avg, dist_t, norm_moba[i].reshape(1, -1),
                                   q_col=3 * heads, k_col=3 * heads + N_HEADS_MOBA,
                                   v_col=3 * heads + 2 * N_HEADS_MOBA, n_heads=N_HEADS_MOBA)
        x1, top_e, gates, rank, counts = _oproj_router(
            mixed_sb.reshape(n_tok, sb_width), mixed_mb.reshape(n_tok, mb_width), x2d,
            w_out[i].astype(BF16), ln1_g[i].reshape(1, -1), ln1_b[i].reshape(1, -1),
            w_router[i], b_router[i].reshape(1, -1), tri)
        _, dest_tiles, row_tok, block_e, n_used = _routing_tables(top_e, rank, counts, n_blocks)
        xs = _sort_gather(row_tok, x1, n_blocks)
        act = _expert_up(block_e, n_used, xs, w_up[i], b_up[i].reshape(n_exp, 1, -1), sel, n_blocks)
        y_sorted = _expert_down(block_e, n_used, act, w_down[i], b_down[i].reshape(n_exp, 1, -1),
                                n_blocks)
        x2d = _combine(dest_tiles, y_sorted, gates, x1,
                       p[i].reshape(n_tok, -1), ln2_g[i].reshape(1, -1), ln2_b[i].reshape(1, -1),
                       w_ple_gate[i].astype(BF16), b_ple_gate[i].reshape(1, -1),
                       w_ple[i].astype(BF16))
    return x2d.reshape(batch, seq, d)
```

```python
import functools
import math

import jax
import jax.numpy as jnp
import numpy as np
from jax import lax
from jax.experimental import pallas as pl
from jax.experimental.pallas import tpu as pltpu

HEAD_DIM = 128
N_HEADS_SB = 8
N_HEADS_MOBA = 8
MOBA_BLOCK = 256
MOBA_TOPK = 3
N_EXPERTS = 32
TOP_K = 4
SWIGLU_LIMIT = 7.0
SWIGLU_ALPHA = 1.702
LN_EPS = 1e-5
RMS_EPS = 1e-6
NEG_INF = -1e30
DEPTH = 1
DEEPNORM_ALPHA = (2.0 * DEPTH) ** 0.25
LOG2_E = math.log2(math.e)

ATT_TILE = 256
ATT_GROUP = 4
SB_QUERY_TILE = 512
SB_GROUP = 2
ROW_BLOCK = 512
TOKEN_TILE = 256
PROJ_TM = 1024
PROJ_TN = 1024
EXPERT_TN = 1024
MXU_DIM = 256
DMA_UNROLL = 8
MIB = 1024 * 1024

F32 = jnp.float32
BF16 = jnp.bfloat16


def _params(semantics, vmem_mib):
    return pltpu.CompilerParams(dimension_semantics=semantics,
                                vmem_limit_bytes=vmem_mib * MIB)


def _split_bf16(a):
    hi = a.astype(BF16)
    lo = (a - hi.astype(F32)).astype(BF16)
    return hi, lo


def _dot(a, b):
    return jnp.dot(a, b, preferred_element_type=F32)


def _dot_nt(a, b):
    return lax.dot_general(a, b, (((1,), (1,)), ((), ())), preferred_element_type=F32)


def _qkv_kernel(x_ref, w_ref, o_ref, xb_ref, *, tile_scales):
    n = pl.program_id(1)

    @pl.when(n == 0)
    def _():
        xb_ref[...] = x_ref[...].astype(BF16)

    scale = jnp.float32(1.0)
    for tile, s in tile_scales:
        scale = jnp.where(n == tile, s, scale)
    o_ref[...] = (_dot(xb_ref[...], w_ref[...]) * scale).astype(o_ref.dtype)


def _qkv_proj(x2d, w_bf16, tile_scales):
    n_tok, d = x2d.shape
    n_out = w_bf16.shape[1]
    tm = min(PROJ_TM, n_tok)
    tn = PROJ_TN
    return pl.pallas_call(
        functools.partial(_qkv_kernel, tile_scales=tile_scales),
        grid=(n_tok // tm, n_out // tn),
        in_specs=[pl.BlockSpec((tm, d), lambda m, n: (m, 0)),
                  pl.BlockSpec((d, tn), lambda m, n: (0, n))],
        out_specs=pl.BlockSpec((tm, tn), lambda m, n: (m, n)),
        out_shape=jax.ShapeDtypeStruct((n_tok, n_out), BF16),
        scratch_shapes=[pltpu.VMEM((tm, d), BF16)],
        compiler_params=_params(("arbitrary", "arbitrary"), 48),
        name="qkv_proj",
    )(x2d, w_bf16)


def _head_rms(o, gain):
    ms = jnp.mean(o * o, axis=-1, keepdims=True)
    return o * lax.rsqrt(ms + RMS_EPS) * gain


def _head_rms_t(o_t, gain):
    ms = jnp.mean(o_t * o_t, axis=0, keepdims=True)
    return (o_t * lax.rsqrt(ms + RMS_EPS)).T * gain


def _store_transposed_values(v_ref, vt_ref, t):
    def body(n, carry):
        v = v_ref[pl.ds(pl.multiple_of(n * t, t), t), :]
        vt_ref[n] = v.astype(F32).T.astype(vt_ref.dtype)
        return carry

    lax.fori_loop(0, vt_ref.shape[0], body, 0)


def _sb_kernel(q_ref, k_ref, v_ref, u_ref, g_ref, o_ref, *, tq, t, group):
    qi = pl.program_id(2)
    ratio = tq // t
    q = q_ref[...]
    u = u_ref[...]
    qry = lax.broadcasted_iota(jnp.int32, (tq, t), 0)
    key = lax.broadcasted_iota(jnp.int32, (tq, t), 1)

    def local(j, past):
        k = k_ref[pl.ds(pl.multiple_of(j * t, t), t), :]
        z = _dot_nt(q, k)
        drop = jnp.maximum(z, 0.0) + jnp.log(1.0 + jnp.exp(-jnp.abs(z)))
        if past is not None:
            drop = jnp.where(past, drop, 0.0)
        hi, lo = _split_bf16(drop)
        between = _dot(hi, u) + _dot(lo, u)
        first = slice(0, HEAD_DIM)
        total = (between[:, first] + drop[:, first])[:, :1]
        return z, drop + between, total

    def finish(j, z, spent, c, acc, past):
        v = v_ref[pl.ds(pl.multiple_of(j * t, t), t), :]
        w = jnp.exp(z - (spent + c))
        if past is not None:
            w = jnp.where(past, w, 0.0)
        return acc + _dot(w.astype(BF16), v)

    def sweep(tiles, c, acc):
        parts = [(j, past) + local(j, past) for j, past in tiles]
        for j, past, z, spent, total in parts:
            acc = finish(j, z, spent, c, acc, past)
            c = c + total
        return c, acc

    diagonal = [(qi * ratio + d, key + d * t < qry) for d in reversed(range(ratio))]
    carry = sweep(diagonal, jnp.zeros((tq, 1), F32), jnp.zeros((tq, HEAD_DIM), F32))

    def body(it, carry):
        first = qi * ratio - 1 - it * group
        return sweep([(first - g, None) for g in range(group)], *carry)

    _, acc = lax.fori_loop(0, qi * (ratio // group), body, carry)
    o_ref[...] = _head_rms(acc, g_ref[...]).astype(o_ref.dtype)


def _sb_attention(qkv, u, gain, *, q_col, k_col, v_col, n_heads):
    b, s, _ = qkv.shape
    t = ATT_TILE
    tq = SB_QUERY_TILE
    assert s % tq == 0 and tq % t == 0 and (tq // t) % SB_GROUP == 0
    return pl.pallas_call(
        functools.partial(_sb_kernel, tq=tq, t=t, group=SB_GROUP),
        grid=(b, n_heads, s // tq),
        in_specs=[pl.BlockSpec((None, tq, HEAD_DIM), lambda b, h, i: (b, i, q_col + h)),
                  pl.BlockSpec((None, s, HEAD_DIM), lambda b, h, i: (b, 0, k_col + h)),
                  pl.BlockSpec((None, s, HEAD_DIM), lambda b, h, i: (b, 0, v_col + h)),
                  pl.BlockSpec((t, t), lambda b, h, i: (0, 0)),
                  pl.BlockSpec((1, HEAD_DIM), lambda b, h, i: (0, h))],
        out_specs=pl.BlockSpec((None, tq, HEAD_DIM), lambda b, h, i: (b, i, h)),
        out_shape=jax.ShapeDtypeStruct((b, s, n_heads * HEAD_DIM), BF16),
        compiler_params=_params(("arbitrary", "arbitrary", "arbitrary"), 32),
        name="sb_attention",
    )(qkv, qkv, qkv, u, gain)


def _moba_kernel(slope_ref, q_ref, k_ref, v_ref, avg_ref, dist_ref, g_ref, o_ref,
                 km_ref, vt_ref, bias_ref, sc_ref, *, t, n_blk, group):
    h = pl.program_id(1)
    qi = pl.program_id(2)

    @pl.when(qi == 0)
    def _():
        km_ref[...] = _dot(avg_ref[...], k_ref[...])
        _store_transposed_values(v_ref, vt_ref, t)

    q = q_ref[...]
    slope = slope_ref[h]
    alibi = slope * dist_ref[...]
    key = lax.broadcasted_iota(jnp.int32, (t, t), 0)
    qry = lax.broadcasted_iota(jnp.int32, (t, t), 1)

    km_hi, km_lo = _split_bf16(km_ref[...])
    gate = _dot_nt(km_hi, q) + _dot_nt(km_lo, q)
    blk = lax.broadcasted_iota(jnp.int32, (n_blk, t), 0)
    valid = blk < qi
    work = jnp.where(valid, gate, -jnp.inf)
    sel = jnp.zeros((n_blk, t), jnp.bool_)
    for _ in range(MOBA_TOPK):
        m = jnp.max(work, axis=0, keepdims=True)
        idx = jnp.min(jnp.where(work == m, blk, n_blk), axis=0, keepdims=True)
        hit = blk == idx
        sel = jnp.logical_or(sel, hit)
        work = jnp.where(hit, -jnp.inf, work)
    bias_ref[...] = jnp.where(jnp.logical_and(sel, valid), 0.0, NEG_INF)

    k = k_ref[pl.ds(pl.multiple_of(qi * t, t), t), :]
    s = jnp.where(key <= qry, _dot_nt(k, q) + alibi, NEG_INF)
    m0 = jnp.max(s, axis=0, keepdims=True)
    p = jnp.exp2(s - m0)
    l0 = jnp.sum(p, axis=0, keepdims=True)
    acc0 = _dot(vt_ref[qi], p.astype(BF16))

    def store_scores(grp, slot):
        for g in range(group):
            j = grp * group + g
            k = k_ref[pl.ds(pl.multiple_of(j * t, t), t), :]
            shift = bias_ref[pl.ds(j, 1), :] - slope * ((qi - j) * t).astype(F32)
            sc_ref[slot, g] = _dot_nt(k, q) + alibi + shift

    def fold(grp, slot, carry):
        m_run, l_run, acc = carry
        scores = [sc_ref[slot, g] for g in range(group)]
        m_new = functools.reduce(
            jnp.maximum, [m_run] + [jnp.max(s, axis=0, keepdims=True) for s in scores])
        a = jnp.exp2(m_run - m_new)
        l_run = a * l_run
        acc = a * acc
        for g, s in enumerate(scores):
            p = jnp.exp2(s - m_new)
            l_run = l_run + jnp.sum(p, axis=0, keepdims=True)
            acc = acc + _dot(vt_ref[grp * group + g], p.astype(BF16))
        return m_new, l_run, acc

    def body(it, carry):
        carry = fold(it, lax.rem(it, 2), carry)
        store_scores(it + 1, lax.rem(it + 1, 2))
        return carry

    last = jnp.maximum((qi + group - 1) // group - 1, 0)
    store_scores(0, 0)
    carry = lax.fori_loop(0, last, body, (m0, l0, acc0))
    _, l_run, acc = fold(last, lax.rem(last, 2), carry)
    o_ref[...] = _head_rms_t(acc / l_run, g_ref[...]).astype(o_ref.dtype)


def _moba_attention(qkv, slopes, avg, dist_t, gain, *, q_col, k_col, v_col, n_heads):
    b, s, _ = qkv.shape
    t = ATT_TILE
    n_blk = s // t
    assert s % t == 0 and n_blk % ATT_GROUP == 0
    grid_spec = pltpu.PrefetchScalarGridSpec(
        num_scalar_prefetch=1,
        grid=(b, n_heads, n_blk),
        in_specs=[pl.BlockSpec((None, t, HEAD_DIM), lambda b, h, i, sl: (b, i, q_col + h)),
                  pl.BlockSpec((None, s, HEAD_DIM), lambda b, h, i, sl: (b, 0, k_col + h)),
                  pl.BlockSpec((None, s, HEAD_DIM), lambda b, h, i, sl: (b, 0, v_col + h)),
                  pl.BlockSpec((n_blk, s), lambda b, h, i, sl: (0, 0)),
                  pl.BlockSpec((t, t), lambda b, h, i, sl: (0, 0)),
                  pl.BlockSpec((1, HEAD_DIM), lambda b, h, i, sl: (0, h))],
        out_specs=pl.BlockSpec((None, t, HEAD_DIM), lambda b, h, i, sl: (b, i, h)),
        scratch_shapes=[pltpu.VMEM((n_blk, HEAD_DIM), F32),
                        pltpu.VMEM((n_blk, HEAD_DIM, t), BF16),
                        pltpu.VMEM((n_blk, t), F32),
                        pltpu.VMEM((2, ATT_GROUP, t, t), F32)])
    return pl.pallas_call(
        functools.partial(_moba_kernel, t=t, n_blk=n_blk, group=ATT_GROUP),
        grid_spec=grid_spec,
        out_shape=jax.ShapeDtypeStruct((b, s, n_heads * HEAD_DIM), BF16),
        compiler_params=_params(("arbitrary", "arbitrary", "arbitrary"), 32),
        name="moba_attention",
    )(slopes, qkv, qkv, qkv, avg, dist_t, gain)


def _layer_norm(x, g, b):
    mu = jnp.mean(x, axis=-1, keepdims=True)
    xc = x - mu
    var = jnp.mean(xc * xc, axis=-1, keepdims=True)
    return xc * lax.rsqrt(var + LN_EPS) * g + b


def _oproj_router_kernel(ms_ref, mm_ref, x_ref, wo_ref, g_ref, b_ref, wr_ref, br_ref, tri_ref,
                         x1_ref, te_ref, gt_ref, rk_ref, cnt_ref, carry_ref, *, sb_width):
    i = pl.program_id(0)

    @pl.when(i == 0)
    def _():
        carry_ref[...] = jnp.zeros_like(carry_ref)

    y = _dot(ms_ref[...], wo_ref[:sb_width, :]) + _dot(mm_ref[...], wo_ref[sb_width:, :])
    x1 = _layer_norm(DEEPNORM_ALPHA * x_ref[...] + y, g_ref[...], b_ref[...])
    x1_ref[...] = x1

    x_hi, x_lo = _split_bf16(x1)
    w_hi, w_lo = _split_bf16(wr_ref[...])
    logits = _dot(x_hi, w_hi) + _dot(x_hi, w_lo) + _dot(x_lo, w_hi) + br_ref[...]

    n_tok, n_exp = logits.shape
    lane = lax.broadcasted_iota(jnp.int32, (n_tok, n_exp), 1)
    work = logits
    vals, hits = [], []
    for _ in range(TOP_K):
        m = jnp.max(work, axis=1, keepdims=True)
        idx = jnp.min(jnp.where(work == m, lane, n_exp), axis=1, keepdims=True)
        hit = lane == idx
        vals.append(m)
        hits.append(hit)
        work = jnp.where(hit, -jnp.inf, work)
    exps = [jnp.exp(v - vals[0]) for v in vals]
    denom = functools.reduce(jnp.add, exps)

    onehot = functools.reduce(jnp.add, [hit.astype(F32) for hit in hits])
    rank_excl = carry_ref[...] + _dot(tri_ref[...], onehot.astype(BF16))
    carry_ref[...] = carry_ref[...] + jnp.sum(onehot, axis=0, keepdims=True)
    cnt_ref[...] = carry_ref[...]

    te_ref[...] = jnp.concatenate(
        [jnp.sum(jnp.where(hit, lane, 0), axis=1, keepdims=True) for hit in hits], axis=1)
    gt_ref[...] = jnp.concatenate([e / denom for e in exps], axis=1)
    rk_ref[...] = jnp.concatenate(
        [jnp.sum(jnp.where(hit, rank_excl, 0.0), axis=1, keepdims=True) for hit in hits],
        axis=1).astype(jnp.int32)


def _oproj_router(mixed_sb, mixed_mb, x2d, w_out, ln_g, ln_b, w_router, b_router, tri):
    n_tok, d = x2d.shape
    sb_width = mixed_sb.shape[1]
    mb_width = mixed_mb.shape[1]
    n_exp = w_router.shape[1]
    t = TOKEN_TILE
    const = lambda i: (0, 0)
    tok = lambda i: (i, 0)
    return pl.pallas_call(
        functools.partial(_oproj_router_kernel, sb_width=sb_width),
        grid=(n_tok // t,),
        in_specs=[pl.BlockSpec((t, sb_width), tok),
                  pl.BlockSpec((t, mb_width), tok),
                  pl.BlockSpec((t, d), tok),
                  pl.BlockSpec((sb_width + mb_width, d), const),
                  pl.BlockSpec((1, d), const),
                  pl.BlockSpec((1, d), const),
                  pl.BlockSpec((d, n_exp), const),
                  pl.BlockSpec((1, n_exp), const),
                  pl.BlockSpec((t, t), const)],
        out_specs=[pl.BlockSpec((t, d), tok),
                   pl.BlockSpec((t, TOP_K), tok),
                   pl.BlockSpec((t, TOP_K), tok),
                   pl.BlockSpec((t, TOP_K), tok),
                   pl.BlockSpec((1, n_exp), const)],
        out_shape=[jax.ShapeDtypeStruct((n_tok, d), F32),
                   jax.ShapeDtypeStruct((n_tok, TOP_K), jnp.int32),
                   jax.ShapeDtypeStruct((n_tok, TOP_K), F32),
                   jax.ShapeDtypeStruct((n_tok, TOP_K), jnp.int32),
                   jax.ShapeDtypeStruct((1, n_exp), F32)],
        scratch_shapes=[pltpu.VMEM((1, n_exp), F32)],
        compiler_params=_params(("arbitrary",), 48),
        name="oproj_ln_router",
    )(mixed_sb, mixed_mb, x2d, w_out, ln_g, ln_b, w_router, b_router, tri)


def _issue_row_copies(n_rows, make_copy):
    def body(it, carry):
        for u in range(DMA_UNROLL):
            make_copy(it * DMA_UNROLL + u).start(priority=u % 2)
        return carry

    lax.fori_loop(0, n_rows // DMA_UNROLL, body, 0)


def _gathered_rows(step, n_steps, idx_hbm, src_hbm, idx_ref, rows_ref, idx_sems, row_sems,
                   count=None):
    n_rows = rows_ref.shape[1]

    def idx_copy(s):
        slot = lax.rem(s, 2)
        return pltpu.make_async_copy(idx_hbm.at[s], idx_ref.at[slot], idx_sems.at[slot])

    def issue(s):
        slot = lax.rem(s, 2)
        idx_copy(s).wait()
        _issue_row_copies(n_rows if count is None else count(s), lambda r: pltpu.make_async_copy(
            src_hbm.at[pl.ds(idx_ref[slot, r], 1)], rows_ref.at[slot, pl.ds(r, 1)],
            row_sems.at[slot]))

    @pl.when(step == 0)
    def _():
        idx_copy(0).start()
        issue(0)
        if n_steps > 1:
            idx_copy(1).start()

    @pl.when(step + 1 < n_steps)
    def _():
        issue(step + 1)

        @pl.when(step + 2 < n_steps)
        def _():
            idx_copy(step + 2).start()

    slot = lax.rem(step, 2)
    if count is None:
        pltpu.make_async_copy(src_hbm.at[pl.ds(0, n_rows)], rows_ref.at[slot],
                              row_sems.at[slot]).wait()
    else:
        n = pl.multiple_of(count(step), DMA_UNROLL)

        @pl.when(n > 0)
        def _():
            pltpu.make_async_copy(src_hbm.at[pl.ds(0, n)], rows_ref.at[slot, pl.ds(0, n)],
                                  row_sems.at[slot]).wait()
    return slot


def _gather_scratch(n_rows, width, dtype):
    return [pltpu.SMEM((2, n_rows), jnp.int32),
            pltpu.VMEM((2, n_rows, width), dtype),
            pltpu.SemaphoreType.DMA((2,)),
            pltpu.SemaphoreType.DMA((2,))]


def _sort_gather_kernel(count_ref, tok_hbm, x_hbm, o_ref, idx_ref, rows_ref, idx_sems, row_sems,
                        *, n_steps):
    b = pl.program_id(0)

    @pl.when(b == 0)
    def _():
        rows_ref[...] = jnp.zeros_like(rows_ref)

    slot = _gathered_rows(b, n_steps, tok_hbm, x_hbm, idx_ref, rows_ref, idx_sems, row_sems,
                          count=lambda s: count_ref[s])
    o_ref[...] = rows_ref[slot].astype(o_ref.dtype)


def _sort_gather(block_rows, row_tok, x1, n_blocks):
    n_tok, d = x1.shape
    rb = ROW_BLOCK
    grid_spec = pltpu.PrefetchScalarGridSpec(
        num_scalar_prefetch=1,
        grid=(n_blocks,),
        in_specs=[pl.BlockSpec(memory_space=pl.ANY), pl.BlockSpec(memory_space=pl.ANY)],
        out_specs=pl.BlockSpec((rb, d), lambda b, cnt: (b, 0)),
        scratch_shapes=_gather_scratch(rb, d, x1.dtype))
    return pl.pallas_call(
        functools.partial(_sort_gather_kernel, n_steps=n_blocks),
        grid_spec=grid_spec,
        out_shape=jax.ShapeDtypeStruct((n_blocks * rb, d), BF16),
        compiler_params=_params(("arbitrary",), 24),
        name="sort_gather",
    )(block_rows, row_tok, x1)


def _expert_changed(be_ref, b):
    prev = be_ref[jnp.maximum(b - 1, 0)]
    return jnp.logical_or(b == 0, be_ref[b] != prev)


def _expert_up_kernel(be_ref, nused_ref, xs_ref, w_ref, bias_ref, sel_ref, o_ref, wb_ref):
    b = pl.program_id(1)

    @pl.when(b >= nused_ref[0])
    def _():
        o_ref[...] = jnp.zeros_like(o_ref)

    @pl.when(b < nused_ref[0])
    def _():
        @pl.when(_expert_changed(be_ref, b))
        def _():
            wb_ref[...] = w_ref[...].astype(BF16)

        hdn = _dot(xs_ref[...], wb_ref[...]) + bias_ref[...]
        tn = hdn.shape[1]
        nxt = pltpu.roll(hdn, tn - 1, 1)
        glu = jnp.minimum(hdn, SWIGLU_LIMIT)
        lin = jnp.clip(nxt, -SWIGLU_LIMIT, SWIGLU_LIMIT)
        act = (glu * jax.nn.sigmoid(SWIGLU_ALPHA * glu) * (lin + 1.0)).astype(BF16)
        sel = sel_ref[...]
        o_ref[...] = jnp.concatenate(
            [_dot(act[:, c:c + MXU_DIM], sel) for c in range(0, tn, MXU_DIM)], axis=1).astype(o_ref.dtype)


def _expert_up(block_e, n_used, xs, w_up, b_up, sel, n_blocks):
    n_rows, d = xs.shape
    n_exp, _, f2 = w_up.shape
    rb = ROW_BLOCK
    tn = EXPERT_TN

    def used(b, nu):
        return jnp.minimum(b, nu[0] - 1)

    grid_spec = pltpu.PrefetchScalarGridSpec(
        num_scalar_prefetch=2,
        grid=(f2 // tn, n_blocks),
        in_specs=[pl.BlockSpec((rb, d), lambda n, b, be, nu: (used(b, nu), 0)),
                  pl.BlockSpec((None, d, tn), lambda n, b, be, nu: (be[used(b, nu)], 0, n)),
                  pl.BlockSpec((None, 1, tn), lambda n, b, be, nu: (be[used(b, nu)], 0, n)),
                  pl.BlockSpec((MXU_DIM, MXU_DIM // 2), lambda n, b, be, nu: (0, 0))],
        out_specs=pl.BlockSpec((rb, tn // 2), lambda n, b, be, nu: (b, n)),
        scratch_shapes=[pltpu.VMEM((d, tn), BF16)])
    return pl.pallas_call(
        _expert_up_kernel,
        grid_spec=grid_spec,
        out_shape=jax.ShapeDtypeStruct((n_rows, f2 // 2), BF16),
        compiler_params=_params(("arbitrary", "arbitrary"), 48),
        name="expert_up",
    )(block_e, n_used, xs, w_up, b_up, sel)


def _expert_down_kernel(be_ref, nused_ref, a_ref, w_ref, bias_ref, o_ref, wb_ref):
    b = pl.program_id(1)

    @pl.when(b >= nused_ref[0])
    def _():
        o_ref[...] = jnp.zeros_like(o_ref)

    @pl.when(b < nused_ref[0])
    def _():
        @pl.when(_expert_changed(be_ref, b))
        def _():
            wb_ref[...] = w_ref[...].astype(BF16)

        o_ref[...] = _dot(a_ref[...], wb_ref[...]) + bias_ref[...]


def _expert_down(block_e, n_used, act, w_down, b_down, n_blocks):
    n_exp, f, d = w_down.shape
    rb = ROW_BLOCK
    tn = EXPERT_TN

    def used(b, nu):
        return jnp.minimum(b, nu[0] - 1)

    grid_spec = pltpu.PrefetchScalarGridSpec(
        num_scalar_prefetch=2,
        grid=(d // tn, n_blocks),
        in_specs=[pl.BlockSpec((rb, f), lambda n, b, be, nu: (used(b, nu), 0)),
                  pl.BlockSpec((None, f, tn), lambda n, b, be, nu: (be[used(b, nu)], 0, n)),
                  pl.BlockSpec((None, 1, tn), lambda n, b, be, nu: (be[used(b, nu)], 0, n))],
        out_specs=pl.BlockSpec((rb, tn), lambda n, b, be, nu: (b, n)),
        scratch_shapes=[pltpu.VMEM((f, tn), BF16)])
    return pl.pallas_call(
        _expert_down_kernel,
        grid_spec=grid_spec,
        out_shape=jax.ShapeDtypeStruct((n_blocks * rb, d), F32),
        compiler_params=_params(("arbitrary", "arbitrary"), 48),
        name="expert_down",
    )(block_e, n_used, act, w_down, b_down)


def _combine_kernel(dest_hbm, y_hbm, gate_ref, x1_ref, p_ref, g_ref, b_ref, wg_ref, bg_ref, wp_ref,
                    o_ref, idx_ref, rows_ref, idx_sems, row_sems, *, t, n_steps):
    slot = _gathered_rows(pl.program_id(0), n_steps, dest_hbm, y_hbm,
                          idx_ref, rows_ref, idx_sems, row_sems)
    gates = gate_ref[...]
    f = functools.reduce(
        jnp.add, [gates[:, k:k + 1] * rows_ref[slot, pl.ds(k * t, t), :] for k in range(TOP_K)])
    x2 = _layer_norm(DEEPNORM_ALPHA * x1_ref[...] + f, g_ref[...], b_ref[...])
    gate = jax.nn.sigmoid(_dot(x2.astype(BF16), wg_ref[...]) + bg_ref[...])
    emb = _dot(p_ref[...].astype(BF16), wp_ref[...])
    o_ref[...] = x2 + gate * emb


def _combine(dest, y_sorted, gates, x1, p2d, ln_g, ln_b, w_gate, b_gate, w_ple):
    n_tok, d = x1.shape
    ple = p2d.shape[1]
    t = TOKEN_TILE
    const = lambda i: (0, 0)
    tok = lambda i: (i, 0)
    return pl.pallas_call(
        functools.partial(_combine_kernel, t=t, n_steps=n_tok // t),
        grid=(n_tok // t,),
        in_specs=[pl.BlockSpec(memory_space=pl.ANY),
                  pl.BlockSpec(memory_space=pl.ANY),
                  pl.BlockSpec((t, TOP_K), tok),
                  pl.BlockSpec((t, d), tok),
                  pl.BlockSpec((t, ple), tok),
                  pl.BlockSpec((1, d), const),
                  pl.BlockSpec((1, d), const),
                  pl.BlockSpec((d, d), const),
                  pl.BlockSpec((1, d), const),
                  pl.BlockSpec((ple, d), const)],
        out_specs=pl.BlockSpec((t, d), tok),
        out_shape=jax.ShapeDtypeStruct((n_tok, d), F32),
        scratch_shapes=_gather_scratch(TOP_K * t, d, y_sorted.dtype),
        compiler_params=_params(("arbitrary",), 56),
        name="combine_ln_ple",
    )(dest, y_sorted, gates, x1, p2d, ln_g, ln_b, w_gate, b_gate, w_ple)


def _routing_tables(top_e, rank, counts, n_blocks):
    n_tok = top_e.shape[0]
    rb = ROW_BLOCK
    counts = counts.reshape(-1).astype(jnp.int32)
    padded = (counts + rb - 1) // rb * rb
    pad_end = jnp.cumsum(padded)
    pad_start = pad_end - padded
    dest = pad_start[top_e] + rank
    tok = jnp.broadcast_to(jnp.arange(n_tok, dtype=jnp.int32)[:, None], dest.shape)
    spread = jnp.arange(n_blocks * rb, dtype=jnp.int32) % n_tok
    row_tok = spread.at[dest.reshape(-1)].set(tok.reshape(-1))
    block_start = jnp.arange(n_blocks, dtype=jnp.int32) * rb
    block_e = jnp.minimum(
        jnp.sum((pad_end[None, :] <= block_start[:, None]).astype(jnp.int32), axis=1),
        counts.shape[0] - 1)
    n_used = (pad_end[-1] // rb).astype(jnp.int32).reshape(1)
    block_rows = jnp.clip(counts[block_e] - (block_start - pad_start[block_e]), 0, rb)
    block_rows = (block_rows + DMA_UNROLL - 1) // DMA_UNROLL * DMA_UNROLL
    dest_tiles = dest.reshape(n_tok // TOKEN_TILE, TOKEN_TILE, TOP_K).transpose(0, 2, 1)
    return (dest, dest_tiles.reshape(n_tok // TOKEN_TILE, TOP_K * TOKEN_TILE),
            row_tok.reshape(n_blocks, rb), block_e, n_used, block_rows.astype(jnp.int32))


def _const_tables(seq):
    t = ATT_TILE
    r = np.arange(t)
    u = (r[:, None] > r[None, :]).astype(np.float32)
    dist_t = (r[:, None] - r[None, :]).astype(np.float32)
    n_blk = seq // MOBA_BLOCK
    avg = np.zeros((n_blk, seq), np.float32)
    for n in range(n_blk):
        avg[n, n * MOBA_BLOCK:(n + 1) * MOBA_BLOCK] = 1.0 / MOBA_BLOCK
    tt = np.arange(TOKEN_TILE)
    tri = (tt[None, :] < tt[:, None]).astype(np.float32)
    sel = np.zeros((MXU_DIM, MXU_DIM // 2), np.float32)
    sel[2 * np.arange(MXU_DIM // 2), np.arange(MXU_DIM // 2)] = 1.0
    slopes = 2.0 ** (-8.0 * np.arange(1, N_HEADS_MOBA + 1) / N_HEADS_MOBA)
    return (jnp.asarray(u, BF16), jnp.asarray(dist_t, F32), jnp.asarray(avg, BF16),
            jnp.asarray(tri, BF16), jnp.asarray(sel, BF16),
            jnp.asarray(slopes * LOG2_E, F32))


def kernel(x, p, w_in, norm_sb, norm_moba, w_out, ln1_g, ln1_b, w_router, b_router, w_up, b_up,
           w_down, b_down, ln2_g, ln2_b, w_ple, w_ple_gate, b_ple_gate):
    batch, seq, d = x.shape
    n_tok = batch * seq
    sb_width = N_HEADS_SB * HEAD_DIM
    mb_width = N_HEADS_MOBA * HEAD_DIM
    n_exp = w_router.shape[-1]
    n_blocks = -(-n_tok * TOP_K // ROW_BLOCK) + n_exp
    u, dist_t, avg, tri, sel, slopes2 = _const_tables(seq)
    heads = sb_width // HEAD_DIM
    x2d = x.reshape(n_tok, d)
    q_scales = ((0, 1.0 / math.sqrt(HEAD_DIM)),
                (3 * sb_width // PROJ_TN, LOG2_E / math.sqrt(HEAD_DIM)))

    for i in range(DEPTH):
        qkv = _qkv_proj(x2d, w_in[i].astype(BF16), q_scales).reshape(batch, seq, -1)
        mixed_sb = _sb_attention(qkv, u, norm_sb[i].reshape(1, -1),
                                 q_col=0, k_col=heads, v_col=2 * heads, n_heads=N_HEADS_SB)
        mixed_mb = _moba_attention(qkv, slopes2, avg, dist_t, norm_moba[i].reshape(1, -1),
                                   q_col=3 * heads, k_col=3 * heads + N_HEADS_MOBA,
                                   v_col=3 * heads + 2 * N_HEADS_MOBA, n_heads=N_HEADS_MOBA)
        x1, top_e, gates, rank, counts = _oproj_router(
            mixed_sb.reshape(n_tok, sb_width), mixed_mb.reshape(n_tok, mb_width), x2d,
            w_out[i].astype(BF16), ln1_g[i].reshape(1, -1), ln1_b[i].reshape(1, -1),
            w_router[i], b_router[i].reshape(1, -1), tri)
        _, dest_tiles, row_tok, block_e, n_used, block_rows = _routing_tables(
            top_e, rank, counts, n_blocks)
        xs = _sort_gather(block_rows, row_tok, x1, n_blocks)
        act = _expert_up(block_e, n_used, xs, w_up[i], b_up[i].reshape(n_exp, 1, -1), sel, n_blocks)
        y_sorted = _expert_down(block_e, n_used, act, w_down[i], b_down[i].reshape(n_exp, 1, -1),
                                n_blocks)
        x2d = _combine(dest_tiles, y_sorted, gates, x1,
                       p[i].reshape(n_tok, -1), ln2_g[i].reshape(1, -1), ln2_b[i].reshape(1, -1),
                       w_ple_gate[i].astype(BF16), b_ple_gate[i].reshape(1, -1),
                       w_ple[i].astype(BF16))
    return x2d.reshape(batch, seq, d)
```

```python
import functools
import math

import jax
import jax.numpy as jnp
import numpy as np
from jax import lax
from jax.experimental import pallas as pl
from jax.experimental.pallas import tpu as pltpu

HEAD_DIM = 128
N_HEADS_SB = 8
N_HEADS_MOBA = 8
MOBA_BLOCK = 256
MOBA_TOPK = 3
N_EXPERTS = 32
TOP_K = 4
SWIGLU_LIMIT = 7.0
SWIGLU_ALPHA = 1.702
LN_EPS = 1e-5
RMS_EPS = 1e-6
NEG_INF = -1e30
DEPTH = 1
DEEPNORM_ALPHA = (2.0 * DEPTH) ** 0.25
LOG2_E = math.log2(math.e)

ATT_TILE = 256
MOBA_QUERY_TILE = 512
MOBA_GROUP = 2
SB_QUERY_TILE = 512
SB_GROUP = 2
ROW_BLOCK = 512
TOKEN_TILE = 256
PROJ_TM = 1024
PROJ_TN = 1024
EXPERT_TN = 1024
MXU_DIM = 256
DMA_UNROLL = 8
MIB = 1024 * 1024

F32 = jnp.float32
BF16 = jnp.bfloat16


def _params(semantics, vmem_mib):
    return pltpu.CompilerParams(dimension_semantics=semantics,
                                vmem_limit_bytes=vmem_mib * MIB)


def _split_bf16(a):
    hi = a.astype(BF16)
    lo = (a - hi.astype(F32)).astype(BF16)
    return hi, lo


def _dot(a, b):
    return jnp.dot(a, b, preferred_element_type=F32)


def _dot_nt(a, b):
    return lax.dot_general(a, b, (((1,), (1,)), ((), ())), preferred_element_type=F32)


def _qkv_kernel(x_ref, w_ref, o_ref, xb_ref, *, tile_scales):
    n = pl.program_id(1)

    @pl.when(n == 0)
    def _():
        xb_ref[...] = x_ref[...].astype(BF16)

    scale = jnp.float32(1.0)
    for tile, s in tile_scales:
        scale = jnp.where(n == tile, s, scale)
    o_ref[...] = (_dot(xb_ref[...], w_ref[...]) * scale).astype(o_ref.dtype)


def _qkv_proj(x2d, w_bf16, tile_scales):
    n_tok, d = x2d.shape
    n_out = w_bf16.shape[1]
    tm = min(PROJ_TM, n_tok)
    tn = PROJ_TN
    return pl.pallas_call(
        functools.partial(_qkv_kernel, tile_scales=tile_scales),
        grid=(n_tok // tm, n_out // tn),
        in_specs=[pl.BlockSpec((tm, d), lambda m, n: (m, 0)),
                  pl.BlockSpec((d, tn), lambda m, n: (0, n))],
        out_specs=pl.BlockSpec((tm, tn), lambda m, n: (m, n)),
        out_shape=jax.ShapeDtypeStruct((n_tok, n_out), BF16),
        scratch_shapes=[pltpu.VMEM((tm, d), BF16)],
        compiler_params=_params(("arbitrary", "arbitrary"), 48),
        name="qkv_proj",
    )(x2d, w_bf16)


def _head_rms(o, gain):
    ms = jnp.mean(o * o, axis=-1, keepdims=True)
    return o * lax.rsqrt(ms + RMS_EPS) * gain


def _head_rms_t(o_t, gain):
    ms = jnp.mean(o_t * o_t, axis=0, keepdims=True)
    return (o_t * lax.rsqrt(ms + RMS_EPS)).T * gain


def _store_transposed_values(v_ref, vt_ref, t):
    def body(n, carry):
        v = v_ref[pl.ds(pl.multiple_of(n * t, t), t), :]
        vt_ref[n] = v.astype(F32).T.astype(vt_ref.dtype)
        return carry

    lax.fori_loop(0, vt_ref.shape[0], body, 0)


def _sb_kernel(q_ref, k_ref, v_ref, u_ref, g_ref, o_ref, *, tq, t, group):
    qi = pl.program_id(2)
    ratio = tq // t
    q = q_ref[...]
    u = u_ref[...]
    qry = lax.broadcasted_iota(jnp.int32, (tq, t), 0)
    key = lax.broadcasted_iota(jnp.int32, (tq, t), 1)

    def local(j, past):
        k = k_ref[pl.ds(pl.multiple_of(j * t, t), t), :]
        z = _dot_nt(q, k)
        drop = jnp.maximum(z, 0.0) + jnp.log(1.0 + jnp.exp(-jnp.abs(z)))
        if past is not None:
            drop = jnp.where(past, drop, 0.0)
        hi, lo = _split_bf16(drop)
        between = _dot(hi, u) + _dot(lo, u)
        first = slice(0, HEAD_DIM)
        total = (between[:, first] + drop[:, first])[:, :1]
        return z, drop + between, total

    def finish(j, z, spent, c, acc, past):
        v = v_ref[pl.ds(pl.multiple_of(j * t, t), t), :]
        w = jnp.exp(z - (spent + c))
        if past is not None:
            w = jnp.where(past, w, 0.0)
        return acc + _dot(w.astype(BF16), v)

    def sweep(tiles, c, acc):
        parts = [(j, past) + local(j, past) for j, past in tiles]
        for j, past, z, spent, total in parts:
            acc = finish(j, z, spent, c, acc, past)
            c = c + total
        return c, acc

    diagonal = [(qi * ratio + d, key + d * t < qry) for d in reversed(range(ratio))]
    carry = sweep(diagonal, jnp.zeros((tq, 1), F32), jnp.zeros((tq, HEAD_DIM), F32))

    def body(it, carry):
        first = qi * ratio - 1 - it * group
        return sweep([(first - g, None) for g in range(group)], *carry)

    _, acc = lax.fori_loop(0, qi * (ratio // group), body, carry)
    o_ref[...] = _head_rms(acc, g_ref[...]).astype(o_ref.dtype)


def _sb_attention(qkv, u, gain, *, q_col, k_col, v_col, n_heads):
    b, s, _ = qkv.shape
    t = ATT_TILE
    tq = SB_QUERY_TILE
    assert s % tq == 0 and tq % t == 0 and (tq // t) % SB_GROUP == 0
    return pl.pallas_call(
        functools.partial(_sb_kernel, tq=tq, t=t, group=SB_GROUP),
        grid=(b, n_heads, s // tq),
        in_specs=[pl.BlockSpec((None, tq, HEAD_DIM), lambda b, h, i: (b, i, q_col + h)),
                  pl.BlockSpec((None, s, HEAD_DIM), lambda b, h, i: (b, 0, k_col + h)),
                  pl.BlockSpec((None, s, HEAD_DIM), lambda b, h, i: (b, 0, v_col + h)),
                  pl.BlockSpec((t, t), lambda b, h, i: (0, 0)),
                  pl.BlockSpec((1, HEAD_DIM), lambda b, h, i: (0, h))],
        out_specs=pl.BlockSpec((None, tq, HEAD_DIM), lambda b, h, i: (b, i, h)),
        out_shape=jax.ShapeDtypeStruct((b, s, n_heads * HEAD_DIM), BF16),
        compiler_params=_params(("arbitrary", "arbitrary", "arbitrary"), 32),
        name="sb_attention",
    )(qkv, qkv, qkv, u, gain)


def _moba_kernel(slope_ref, q_ref, k_ref, v_ref, avg_ref, dist_ref, g_ref, o_ref,
                 km_ref, vt_ref, bias_ref, sc_ref, *, tq, t, n_blk, group):
    h = pl.program_id(1)
    qi = pl.program_id(2)
    ratio = tq // t

    @pl.when(qi == 0)
    def _():
        km_ref[...] = _dot(avg_ref[...], k_ref[...])
        _store_transposed_values(v_ref, vt_ref, t)

    q = q_ref[...]
    slope = slope_ref[h]
    alibi = slope * dist_ref[...]
    key = lax.broadcasted_iota(jnp.int32, (t, tq), 0)
    qry = lax.broadcasted_iota(jnp.int32, (t, tq), 1)
    lane = lax.broadcasted_iota(jnp.int32, (1, tq), 1)
    sub = functools.reduce(jnp.add, [(lane >= d * t).astype(jnp.int32) for d in range(1, ratio)])
    own = qi * ratio + sub

    def tile(j):
        k = k_ref[pl.ds(pl.multiple_of(j * t, t), t), :]
        return _dot_nt(k, q) + alibi - slope * (qi * tq - j * t).astype(F32)

    km_hi, km_lo = _split_bf16(km_ref[...])
    gate = _dot_nt(km_hi, q) + _dot_nt(km_lo, q)
    blk = lax.broadcasted_iota(jnp.int32, (n_blk, tq), 0)
    valid = blk < own
    work = jnp.where(valid, gate, -jnp.inf)
    sel = jnp.zeros((n_blk, tq), jnp.bool_)
    for _ in range(MOBA_TOPK):
        m = jnp.max(work, axis=0, keepdims=True)
        idx = jnp.min(jnp.where(work == m, blk, n_blk), axis=0, keepdims=True)
        hit = blk == idx
        sel = jnp.logical_or(sel, hit)
        work = jnp.where(hit, -jnp.inf, work)
    bias_ref[...] = jnp.where(jnp.logical_and(sel, valid), 0.0, NEG_INF)

    scores = []
    for d in range(ratio):
        j = qi * ratio + d
        mask = jnp.where(sub > d, bias_ref[pl.ds(j, 1), :],
                         jnp.where(jnp.logical_and(sub == d, key + d * t <= qry), 0.0, NEG_INF))
        scores.append(tile(j) + mask)
    m0 = functools.reduce(jnp.maximum, [jnp.max(s, axis=0, keepdims=True) for s in scores])
    l0 = jnp.zeros((1, tq), F32)
    acc0 = jnp.zeros((HEAD_DIM, tq), F32)
    for d, s in enumerate(scores):
        p = jnp.exp2(s - m0)
        l0 = l0 + jnp.sum(p, axis=0, keepdims=True)
        acc0 = acc0 + _dot(vt_ref[qi * ratio + d], p.astype(BF16))

    n_groups = qi * (ratio // group)
    dead = jnp.where(n_groups > 0, 0.0, NEG_INF)

    def store_scores(grp, slot):
        for g in range(group):
            j = grp * group + g
            sc_ref[slot, g] = tile(j) + (bias_ref[pl.ds(j, 1), :] + dead)

    def fold(grp, slot, carry):
        m_run, l_run, acc = carry
        scores = [sc_ref[slot, g] for g in range(group)]
        m_new = functools.reduce(
            jnp.maximum, [m_run] + [jnp.max(s, axis=0, keepdims=True) for s in scores])
        a = jnp.exp2(m_run - m_new)
        l_run = a * l_run
        acc = a * acc
        for g, s in enumerate(scores):
            p = jnp.exp2(s - m_new)
            l_run = l_run + jnp.sum(p, axis=0, keepdims=True)
            acc = acc + _dot(vt_ref[grp * group + g], p.astype(BF16))
        return m_new, l_run, acc

    def body(it, carry):
        carry = fold(it, lax.rem(it, 2), carry)
        store_scores(it + 1, lax.rem(it + 1, 2))
        return carry

    last = jnp.maximum(n_groups - 1, 0)
    store_scores(0, 0)
    carry = lax.fori_loop(0, last, body, (m0, l0, acc0))
    _, l_run, acc = fold(last, lax.rem(last, 2), carry)
    o_ref[...] = _head_rms_t(acc / l_run, g_ref[...]).astype(o_ref.dtype)


def _moba_attention(qkv, slopes, avg, dist_t, gain, *, q_col, k_col, v_col, n_heads):
    b, s, _ = qkv.shape
    t = MOBA_BLOCK
    tq = MOBA_QUERY_TILE
    group = MOBA_GROUP
    n_blk = s // t
    assert s % tq == 0 and tq % t == 0 and (tq // t) % group == 0
    grid_spec = pltpu.PrefetchScalarGridSpec(
        num_scalar_prefetch=1,
        grid=(b, n_heads, s // tq),
        in_specs=[pl.BlockSpec((None, tq, HEAD_DIM), lambda b, h, i, sl: (b, i, q_col + h)),
                  pl.BlockSpec((None, s, HEAD_DIM), lambda b, h, i, sl: (b, 0, k_col + h)),
                  pl.BlockSpec((None, s, HEAD_DIM), lambda b, h, i, sl: (b, 0, v_col + h)),
                  pl.BlockSpec((n_blk, s), lambda b, h, i, sl: (0, 0)),
                  pl.BlockSpec((t, tq), lambda b, h, i, sl: (0, 0)),
                  pl.BlockSpec((1, HEAD_DIM), lambda b, h, i, sl: (0, h))],
        out_specs=pl.BlockSpec((None, tq, HEAD_DIM), lambda b, h, i, sl: (b, i, h)),
        scratch_shapes=[pltpu.VMEM((n_blk, HEAD_DIM), F32),
                        pltpu.VMEM((n_blk, HEAD_DIM, t), BF16),
                        pltpu.VMEM((n_blk, tq), F32),
                        pltpu.VMEM((2, group, t, tq), F32)])
    return pl.pallas_call(
        functools.partial(_moba_kernel, tq=tq, t=t, n_blk=n_blk, group=group),
        grid_spec=grid_spec,
        out_shape=jax.ShapeDtypeStruct((b, s, n_heads * HEAD_DIM), BF16),
        compiler_params=_params(("arbitrary", "arbitrary", "arbitrary"), 32),
        name="moba_attention",
    )(slopes, qkv, qkv, qkv, avg, dist_t, gain)


def _layer_norm(x, g, b):
    mu = jnp.mean(x, axis=-1, keepdims=True)
    xc = x - mu
    var = jnp.mean(xc * xc, axis=-1, keepdims=True)
    return xc * lax.rsqrt(var + LN_EPS) * g + b


def _oproj_router_kernel(ms_ref, mm_ref, x_ref, wo_ref, g_ref, b_ref, wr_ref, br_ref, tri_ref,
                         x1_ref, te_ref, gt_ref, rk_ref, cnt_ref, carry_ref, *, sb_width):
    i = pl.program_id(0)

    @pl.when(i == 0)
    def _():
        carry_ref[...] = jnp.zeros_like(carry_ref)

    y = _dot(ms_ref[...], wo_ref[:sb_width, :]) + _dot(mm_ref[...], wo_ref[sb_width:, :])
    x1 = _layer_norm(DEEPNORM_ALPHA * x_ref[...] + y, g_ref[...], b_ref[...])
    x1_ref[...] = x1

    x_hi, x_lo = _split_bf16(x1)
    w_hi, w_lo = _split_bf16(wr_ref[...])
    logits = _dot(x_hi, w_hi) + _dot(x_hi, w_lo) + _dot(x_lo, w_hi) + br_ref[...]

    n_tok, n_exp = logits.shape
    lane = lax.broadcasted_iota(jnp.int32, (n_tok, n_exp), 1)
    work = logits
    vals, hits = [], []
    for _ in range(TOP_K):
        m = jnp.max(work, axis=1, keepdims=True)
        idx = jnp.min(jnp.where(work == m, lane, n_exp), axis=1, keepdims=True)
        hit = lane == idx
        vals.append(m)
        hits.append(hit)
        work = jnp.where(hit, -jnp.inf, work)
    exps = [jnp.exp(v - vals[0]) for v in vals]
    denom = functools.reduce(jnp.add, exps)

    onehot = functools.reduce(jnp.add, [hit.astype(F32) for hit in hits])
    rank_excl = carry_ref[...] + _dot(tri_ref[...], onehot.astype(BF16))
    carry_ref[...] = carry_ref[...] + jnp.sum(onehot, axis=0, keepdims=True)
    cnt_ref[...] = carry_ref[...]

    te_ref[...] = jnp.concatenate(
        [jnp.sum(jnp.where(hit, lane, 0), axis=1, keepdims=True) for hit in hits], axis=1)
    gt_ref[...] = jnp.concatenate([e / denom for e in exps], axis=1)
    rk_ref[...] = jnp.concatenate(
        [jnp.sum(jnp.where(hit, rank_excl, 0.0), axis=1, keepdims=True) for hit in hits],
        axis=1).astype(jnp.int32)


def _oproj_router(mixed_sb, mixed_mb, x2d, w_out, ln_g, ln_b, w_router, b_router, tri):
    n_tok, d = x2d.shape
    sb_width = mixed_sb.shape[1]
    mb_width = mixed_mb.shape[1]
    n_exp = w_router.shape[1]
    t = TOKEN_TILE
    const = lambda i: (0, 0)
    tok = lambda i: (i, 0)
    return pl.pallas_call(
        functools.partial(_oproj_router_kernel, sb_width=sb_width),
        grid=(n_tok // t,),
        in_specs=[pl.BlockSpec((t, sb_width), tok),
                  pl.BlockSpec((t, mb_width), tok),
                  pl.BlockSpec((t, d), tok),
                  pl.BlockSpec((sb_width + mb_width, d), const),
                  pl.BlockSpec((1, d), const),
                  pl.BlockSpec((1, d), const),
                  pl.BlockSpec((d, n_exp), const),
                  pl.BlockSpec((1, n_exp), const),
                  pl.BlockSpec((t, t), const)],
        out_specs=[pl.BlockSpec((t, d), tok),
                   pl.BlockSpec((t, TOP_K), tok),
                   pl.BlockSpec((t, TOP_K), tok),
                   pl.BlockSpec((t, TOP_K), tok),
                   pl.BlockSpec((1, n_exp), const)],
        out_shape=[jax.ShapeDtypeStruct((n_tok, d), F32),
                   jax.ShapeDtypeStruct((n_tok, TOP_K), jnp.int32),
                   jax.ShapeDtypeStruct((n_tok, TOP_K), F32),
                   jax.ShapeDtypeStruct((n_tok, TOP_K), jnp.int32),
                   jax.ShapeDtypeStruct((1, n_exp), F32)],
        scratch_shapes=[pltpu.VMEM((1, n_exp), F32)],
        compiler_params=_params(("arbitrary",), 48),
        name="oproj_ln_router",
    )(mixed_sb, mixed_mb, x2d, w_out, ln_g, ln_b, w_router, b_router, tri)


def _issue_row_copies(n_rows, make_copy):
    def body(it, carry):
        for u in range(DMA_UNROLL):
            make_copy(it * DMA_UNROLL + u).start(priority=u % 2)
        return carry

    lax.fori_loop(0, n_rows // DMA_UNROLL, body, 0)


def _gathered_rows(step, n_steps, idx_hbm, src_hbm, idx_ref, rows_ref, idx_sems, row_sems,
                   count=None):
    n_rows = rows_ref.shape[1]

    def idx_copy(s):
        slot = lax.rem(s, 2)
        return pltpu.make_async_copy(idx_hbm.at[s], idx_ref.at[slot], idx_sems.at[slot])

    def issue(s):
        slot = lax.rem(s, 2)
        idx_copy(s).wait()
        _issue_row_copies(n_rows if count is None else count(s), lambda r: pltpu.make_async_copy(
            src_hbm.at[pl.ds(idx_ref[slot, r], 1)], rows_ref.at[slot, pl.ds(r, 1)],
            row_sems.at[slot]))

    @pl.when(step == 0)
    def _():
        idx_copy(0).start()
        issue(0)
        if n_steps > 1:
            idx_copy(1).start()

    @pl.when(step + 1 < n_steps)
    def _():
        issue(step + 1)

        @pl.when(step + 2 < n_steps)
        def _():
            idx_copy(step + 2).start()

    slot = lax.rem(step, 2)
    if count is None:
        pltpu.make_async_copy(src_hbm.at[pl.ds(0, n_rows)], rows_ref.at[slot],
                              row_sems.at[slot]).wait()
    else:
        n = pl.multiple_of(count(step), DMA_UNROLL)

        @pl.when(n > 0)
        def _():
            pltpu.make_async_copy(src_hbm.at[pl.ds(0, n)], rows_ref.at[slot, pl.ds(0, n)],
                                  row_sems.at[slot]).wait()
    return slot


def _gather_scratch(n_rows, width, dtype):
    return [pltpu.SMEM((2, n_rows), jnp.int32),
            pltpu.VMEM((2, n_rows, width), dtype),
            pltpu.SemaphoreType.DMA((2,)),
            pltpu.SemaphoreType.DMA((2,))]


def _sort_gather_kernel(count_ref, tok_hbm, x_hbm, o_ref, idx_ref, rows_ref, idx_sems, row_sems,
                        *, n_steps):
    b = pl.program_id(0)

    @pl.when(b == 0)
    def _():
        rows_ref[...] = jnp.zeros_like(rows_ref)

    slot = _gathered_rows(b, n_steps, tok_hbm, x_hbm, idx_ref, rows_ref, idx_sems, row_sems,
                          count=lambda s: count_ref[s])
    o_ref[...] = rows_ref[slot].astype(o_ref.dtype)


def _sort_gather(block_rows, row_tok, x1, n_blocks):
    n_tok, d = x1.shape
    rb = ROW_BLOCK
    grid_spec = pltpu.PrefetchScalarGridSpec(
        num_scalar_prefetch=1,
        grid=(n_blocks,),
        in_specs=[pl.BlockSpec(memory_space=pl.ANY), pl.BlockSpec(memory_space=pl.ANY)],
        out_specs=pl.BlockSpec((rb, d), lambda b, cnt: (b, 0)),
        scratch_shapes=_gather_scratch(rb, d, x1.dtype))
    return pl.pallas_call(
        functools.partial(_sort_gather_kernel, n_steps=n_blocks),
        grid_spec=grid_spec,
        out_shape=jax.ShapeDtypeStruct((n_blocks * rb, d), BF16),
        compiler_params=_params(("arbitrary",), 24),
        name="sort_gather",
    )(block_rows, row_tok, x1)


def _expert_changed(be_ref, b):
    prev = be_ref[jnp.maximum(b - 1, 0)]
    return jnp.logical_or(b == 0, be_ref[b] != prev)


def _expert_up_kernel(be_ref, nused_ref, xs_ref, w_ref, bias_ref, sel_ref, o_ref, wb_ref):
    b = pl.program_id(1)

    @pl.when(b >= nused_ref[0])
    def _():
        o_ref[...] = jnp.zeros_like(o_ref)

    @pl.when(b < nused_ref[0])
    def _():
        @pl.when(_expert_changed(be_ref, b))
        def _():
            wb_ref[...] = w_ref[...].astype(BF16)

        hdn = _dot(xs_ref[...], wb_ref[...]) + bias_ref[...]
        tn = hdn.shape[1]
        nxt = pltpu.roll(hdn, tn - 1, 1)
        glu = jnp.minimum(hdn, SWIGLU_LIMIT)
        lin = jnp.clip(nxt, -SWIGLU_LIMIT, SWIGLU_LIMIT)
        act = (glu * jax.nn.sigmoid(SWIGLU_ALPHA * glu) * (lin + 1.0)).astype(BF16)
        sel = sel_ref[...]
        o_ref[...] = jnp.concatenate(
            [_dot(act[:, c:c + MXU_DIM], sel) for c in range(0, tn, MXU_DIM)], axis=1).astype(o_ref.dtype)


def _expert_up(block_e, n_used, xs, w_up, b_up, sel, n_blocks):
    n_rows, d = xs.shape
    n_exp, _, f2 = w_up.shape
    rb = ROW_BLOCK
    tn = EXPERT_TN

    def used(b, nu):
        return jnp.minimum(b, nu[0] - 1)

    grid_spec = pltpu.PrefetchScalarGridSpec(
        num_scalar_prefetch=2,
        grid=(f2 // tn, n_blocks),
        in_specs=[pl.BlockSpec((rb, d), lambda n, b, be, nu: (used(b, nu), 0)),
                  pl.BlockSpec((None, d, tn), lambda n, b, be, nu: (be[used(b, nu)], 0, n)),
                  pl.BlockSpec((None, 1, tn), lambda n, b, be, nu: (be[used(b, nu)], 0, n)),
                  pl.BlockSpec((MXU_DIM, MXU_DIM // 2), lambda n, b, be, nu: (0, 0))],
        out_specs=pl.BlockSpec((rb, tn // 2), lambda n, b, be, nu: (b, n)),
        scratch_shapes=[pltpu.VMEM((d, tn), BF16)])
    return pl.pallas_call(
        _expert_up_kernel,
        grid_spec=grid_spec,
        out_shape=jax.ShapeDtypeStruct((n_rows, f2 // 2), BF16),
        compiler_params=_params(("arbitrary", "arbitrary"), 48),
        name="expert_up",
    )(block_e, n_used, xs, w_up, b_up, sel)


def _expert_down_kernel(be_ref, nused_ref, a_ref, w_ref, bias_ref, o_ref, wb_ref):
    b = pl.program_id(1)

    @pl.when(b >= nused_ref[0])
    def _():
        o_ref[...] = jnp.zeros_like(o_ref)

    @pl.when(b < nused_ref[0])
    def _():
        @pl.when(_expert_changed(be_ref, b))
        def _():
            wb_ref[...] = w_ref[...].astype(BF16)

        o_ref[...] = _dot(a_ref[...], wb_ref[...]) + bias_ref[...]


def _expert_down(block_e, n_used, act, w_down, b_down, n_blocks):
    n_exp, f, d = w_down.shape
    rb = ROW_BLOCK
    tn = EXPERT_TN

    def used(b, nu):
        return jnp.minimum(b, nu[0] - 1)

    grid_spec = pltpu.PrefetchScalarGridSpec(
        num_scalar_prefetch=2,
        grid=(d // tn, n_blocks),
        in_specs=[pl.BlockSpec((rb, f), lambda n, b, be, nu: (used(b, nu), 0)),
                  pl.BlockSpec((None, f, tn), lambda n, b, be, nu: (be[used(b, nu)], 0, n)),
                  pl.BlockSpec((None, 1, tn), lambda n, b, be, nu: (be[used(b, nu)], 0, n))],
        out_specs=pl.BlockSpec((rb, tn), lambda n, b, be, nu: (b, n)),
        scratch_shapes=[pltpu.VMEM((f, tn), BF16)])
    return pl.pallas_call(
        _expert_down_kernel,
        grid_spec=grid_spec,
        out_shape=jax.ShapeDtypeStruct((n_blocks * rb, d), F32),
        compiler_params=_params(("arbitrary", "arbitrary"), 48),
        name="expert_down",
    )(block_e, n_used, act, w_down, b_down)


def _combine_kernel(dest_hbm, y_hbm, gate_ref, x1_ref, p_ref, g_ref, b_ref, wg_ref, bg_ref, wp_ref,
                    o_ref, idx_ref, rows_ref, idx_sems, row_sems, *, t, n_steps):
    slot = _gathered_rows(pl.program_id(0), n_steps, dest_hbm, y_hbm,
                          idx_ref, rows_ref, idx_sems, row_sems)
    gates = gate_ref[...]
    f = functools.reduce(
        jnp.add, [gates[:, k:k + 1] * rows_ref[slot, pl.ds(k * t, t), :] for k in range(TOP_K)])
    x2 = _layer_norm(DEEPNORM_ALPHA * x1_ref[...] + f, g_ref[...], b_ref[...])
    gate = jax.nn.sigmoid(_dot(x2.astype(BF16), wg_ref[...]) + bg_ref[...])
    emb = _dot(p_ref[...].astype(BF16), wp_ref[...])
    o_ref[...] = x2 + gate * emb


def _combine(dest, y_sorted, gates, x1, p2d, ln_g, ln_b, w_gate, b_gate, w_ple):
    n_tok, d = x1.shape
    ple = p2d.shape[1]
    t = TOKEN_TILE
    const = lambda i: (0, 0)
    tok = lambda i: (i, 0)
    return pl.pallas_call(
        functools.partial(_combine_kernel, t=t, n_steps=n_tok // t),
        grid=(n_tok // t,),
        in_specs=[pl.BlockSpec(memory_space=pl.ANY),
                  pl.BlockSpec(memory_space=pl.ANY),
                  pl.BlockSpec((t, TOP_K), tok),
                  pl.BlockSpec((t, d), tok),
                  pl.BlockSpec((t, ple), tok),
                  pl.BlockSpec((1, d), const),
                  pl.BlockSpec((1, d), const),
                  pl.BlockSpec((d, d), const),
                  pl.BlockSpec((1, d), const),
                  pl.BlockSpec((ple, d), const)],
        out_specs=pl.BlockSpec((t, d), tok),
        out_shape=jax.ShapeDtypeStruct((n_tok, d), F32),
        scratch_shapes=_gather_scratch(TOP_K * t, d, y_sorted.dtype),
        compiler_params=_params(("arbitrary",), 56),
        name="combine_ln_ple",
    )(dest, y_sorted, gates, x1, p2d, ln_g, ln_b, w_gate, b_gate, w_ple)


def _routing_tables(top_e, rank, counts, n_blocks):
    n_tok = top_e.shape[0]
    rb = ROW_BLOCK
    counts = counts.reshape(-1).astype(jnp.int32)
    padded = (counts + rb - 1) // rb * rb
    pad_end = jnp.cumsum(padded)
    pad_start = pad_end - padded
    dest = pad_start[top_e] + rank
    tok = jnp.broadcast_to(jnp.arange(n_tok, dtype=jnp.int32)[:, None], dest.shape)
    spread = jnp.arange(n_blocks * rb, dtype=jnp.int32) % n_tok
    row_tok = spread.at[dest.reshape(-1)].set(tok.reshape(-1), unique_indices=True)
    block_start = jnp.arange(n_blocks, dtype=jnp.int32) * rb
    block_e = jnp.minimum(
        jnp.sum((pad_end[None, :] <= block_start[:, None]).astype(jnp.int32), axis=1),
        counts.shape[0] - 1)
    n_used = (pad_end[-1] // rb).astype(jnp.int32).reshape(1)
    block_rows = jnp.clip(counts[block_e] - (block_start - pad_start[block_e]), 0, rb)
    block_rows = (block_rows + DMA_UNROLL - 1) // DMA_UNROLL * DMA_UNROLL
    dest_tiles = dest.reshape(n_tok // TOKEN_TILE, TOKEN_TILE, TOP_K).transpose(0, 2, 1)
    return (dest, dest_tiles.reshape(n_tok // TOKEN_TILE, TOP_K * TOKEN_TILE),
            row_tok.reshape(n_blocks, rb), block_e, n_used, block_rows.astype(jnp.int32))


def _const_tables(seq):
    t = ATT_TILE
    r = np.arange(t)
    u = (r[:, None] > r[None, :]).astype(np.float32)
    dist_t = (np.arange(MOBA_BLOCK)[:, None]
              - np.arange(MOBA_QUERY_TILE)[None, :]).astype(np.float32)
    n_blk = seq // MOBA_BLOCK
    avg = np.zeros((n_blk, seq), np.float32)
    for n in range(n_blk):
        avg[n, n * MOBA_BLOCK:(n + 1) * MOBA_BLOCK] = 1.0 / MOBA_BLOCK
    tt = np.arange(TOKEN_TILE)
    tri = (tt[None, :] < tt[:, None]).astype(np.float32)
    sel = np.zeros((MXU_DIM, MXU_DIM // 2), np.float32)
    sel[2 * np.arange(MXU_DIM // 2), np.arange(MXU_DIM // 2)] = 1.0
    slopes = 2.0 ** (-8.0 * np.arange(1, N_HEADS_MOBA + 1) / N_HEADS_MOBA)
    return (jnp.asarray(u, BF16), jnp.asarray(dist_t, F32), jnp.asarray(avg, BF16),
            jnp.asarray(tri, BF16), jnp.asarray(sel, BF16),
            jnp.asarray(slopes * LOG2_E, F32))


def kernel(x, p, w_in, norm_sb, norm_moba, w_out, ln1_g, ln1_b, w_router, b_router, w_up, b_up,
           w_down, b_down, ln2_g, ln2_b, w_ple, w_ple_gate, b_ple_gate):
    batch, seq, d = x.shape
    n_tok = batch * seq
    sb_width = N_HEADS_SB * HEAD_DIM
    mb_width = N_HEADS_MOBA * HEAD_DIM
    n_exp = w_router.shape[-1]
    n_blocks = -(-n_tok * TOP_K // ROW_BLOCK) + n_exp
    u, dist_t, avg, tri, sel, slopes2 = _const_tables(seq)
    heads = sb_width // HEAD_DIM
    x2d = x.reshape(n_tok, d)
    q_scales = ((0, 1.0 / math.sqrt(HEAD_DIM)),
                (3 * sb_width // PROJ_TN, LOG2_E / math.sqrt(HEAD_DIM)))

    for i in range(DEPTH):
        qkv = _qkv_proj(x2d, w_in[i].astype(BF16), q_scales).reshape(batch, seq, -1)
        mixed_sb = _sb_attention(qkv, u, norm_sb[i].reshape(1, -1),
                                 q_col=0, k_col=heads, v_col=2 * heads, n_heads=N_HEADS_SB)
        mixed_mb = _moba_attention(qkv, slopes2, avg, dist_t, norm_moba[i].reshape(1, -1),
                                   q_col=3 * heads, k_col=3 * heads + N_HEADS_MOBA,
                                   v_col=3 * heads + 2 * N_HEADS_MOBA, n_heads=N_HEADS_MOBA)
        x1, top_e, gates, rank, counts = _oproj_router(
            mixed_sb.reshape(n_tok, sb_width), mixed_mb.reshape(n_tok, mb_width), x2d,
            w_out[i].astype(BF16), ln1_g[i].reshape(1, -1), ln1_b[i].reshape(1, -1),
            w_router[i], b_router[i].reshape(1, -1), tri)
        _, dest_tiles, row_tok, block_e, n_used, block_rows = _routing_tables(
            top_e, rank, counts, n_blocks)
        xs = _sort_gather(block_rows, row_tok, x1, n_blocks)
        act = _expert_up(block_e, n_used, xs, w_up[i], b_up[i].reshape(n_exp, 1, -1), sel, n_blocks)
        y_sorted = _expert_down(block_e, n_used, act, w_down[i], b_down[i].reshape(n_exp, 1, -1),
                                n_blocks)
        x2d = _combine(dest_tiles, y_sorted, gates, x1,
                       p[i].reshape(n_tok, -1), ln2_g[i].reshape(1, -1), ln2_b[i].reshape(1, -1),
                       w_ple_gate[i].astype(BF16), b_ple_gate[i].reshape(1, -1),
                       w_ple[i].astype(BF16))
    return x2d.reshape(batch, seq, d)
```

```python
import functools
import math

import jax
import jax.numpy as jnp
import numpy as np
from jax import lax
from jax.experimental import pallas as pl
from jax.experimental.pallas import tpu as pltpu

HEAD_DIM = 128
N_HEADS_SB = 8
N_HEADS_MOBA = 8
MOBA_BLOCK = 256
MOBA_TOPK = 3
N_EXPERTS = 32
TOP_K = 4
SWIGLU_LIMIT = 7.0
SWIGLU_ALPHA = 1.702
LN_EPS = 1e-5
RMS_EPS = 1e-6
NEG_INF = -1e30
DEPTH = 1
DEEPNORM_ALPHA = (2.0 * DEPTH) ** 0.25
LOG2_E = math.log2(math.e)
EXP_UNDERFLOW = 110.0

ATT_TILE = 256
MOBA_QUERY_TILE = 512
MOBA_GROUP = 2
SB_QUERY_TILE = 512
SB_GROUP = 2
ROW_BLOCK = 512
TOKEN_TILE = 256
PROJ_TM = 1024
PROJ_TN = 1024
EXPERT_TN = 1024
MXU_DIM = 256
DMA_UNROLL = 8
MIB = 1024 * 1024

F32 = jnp.float32
BF16 = jnp.bfloat16


def _params(semantics, vmem_mib):
    return pltpu.CompilerParams(dimension_semantics=semantics,
                                vmem_limit_bytes=vmem_mib * MIB)


def _split_bf16(a):
    hi = a.astype(BF16)
    lo = (a - hi.astype(F32)).astype(BF16)
    return hi, lo


def _dot(a, b):
    return jnp.dot(a, b, preferred_element_type=F32)


def _dot_nt(a, b):
    return lax.dot_general(a, b, (((1,), (1,)), ((), ())), preferred_element_type=F32)


def _qkv_kernel(x_ref, w_ref, o_ref, xb_ref, *, tile_scales):
    n = pl.program_id(1)

    @pl.when(n == 0)
    def _():
        xb_ref[...] = x_ref[...].astype(BF16)

    scale = jnp.float32(1.0)
    for tile, s in tile_scales:
        scale = jnp.where(n == tile, s, scale)
    o_ref[...] = (_dot(xb_ref[...], w_ref[...]) * scale).astype(o_ref.dtype)


def _qkv_proj(x2d, w_bf16, tile_scales):
    n_tok, d = x2d.shape
    n_out = w_bf16.shape[1]
    tm = min(PROJ_TM, n_tok)
    tn = PROJ_TN
    return pl.pallas_call(
        functools.partial(_qkv_kernel, tile_scales=tile_scales),
        grid=(n_tok // tm, n_out // tn),
        in_specs=[pl.BlockSpec((tm, d), lambda m, n: (m, 0)),
                  pl.BlockSpec((d, tn), lambda m, n: (0, n))],
        out_specs=pl.BlockSpec((tm, tn), lambda m, n: (m, n)),
        out_shape=jax.ShapeDtypeStruct((n_tok, n_out), BF16),
        scratch_shapes=[pltpu.VMEM((tm, d), BF16)],
        compiler_params=_params(("arbitrary", "arbitrary"), 48),
        name="qkv_proj",
    )(x2d, w_bf16)


def _head_rms(o, gain):
    ms = jnp.mean(o * o, axis=-1, keepdims=True)
    return o * lax.rsqrt(ms + RMS_EPS) * gain


def _head_rms_t(o_t, gain):
    ms = jnp.mean(o_t * o_t, axis=0, keepdims=True)
    return (o_t * lax.rsqrt(ms + RMS_EPS)).T * gain


def _store_transposed_values(v_ref, vt_ref, t):
    def body(n, carry):
        v = v_ref[pl.ds(pl.multiple_of(n * t, t), t), :]
        vt_ref[n] = v.astype(F32).T.astype(vt_ref.dtype)
        return carry

    lax.fori_loop(0, vt_ref.shape[0], body, 0)


def _sb_kernel(q_ref, k_ref, v_ref, u_ref, g_ref, o_ref, knorm_ref, *, tq, t, group):
    qi = pl.program_id(2)
    ratio = tq // t

    @pl.when(qi == 0)
    def _():
        def body(n, m):
            k = k_ref[pl.ds(pl.multiple_of(n * t, t), t), :].astype(F32)
            return jnp.maximum(m, jnp.max(jnp.sum(k * k, axis=1, keepdims=True), axis=0, keepdims=True))

        knorm_ref[...] = jnp.sqrt(lax.fori_loop(0, k_ref.shape[0] // t, body, jnp.zeros((1, 1), F32)))

    q = q_ref[...]
    u = u_ref[...]
    qry = lax.broadcasted_iota(jnp.int32, (tq, t), 0)
    key = lax.broadcasted_iota(jnp.int32, (tq, t), 1)
    qf = q.astype(F32)
    z_bound = jnp.sqrt(jnp.sum(qf * qf, axis=1, keepdims=True)) * knorm_ref[...] * 1.001 + 1e-3

    def local(j, past):
        k = k_ref[pl.ds(pl.multiple_of(j * t, t), t), :]
        z = _dot_nt(q, k)
        drop = jnp.maximum(z, 0.0) + jnp.log(1.0 + jnp.exp(-jnp.abs(z)))
        if past is not None:
            drop = jnp.where(past, drop, 0.0)
        hi, lo = _split_bf16(drop)
        between = _dot(hi, u) + _dot(lo, u)
        first = slice(0, HEAD_DIM)
        total = (between[:, first] + drop[:, first])[:, :1]
        return z, drop + between, total

    def finish(j, z, spent, c, acc, past):
        v = v_ref[pl.ds(pl.multiple_of(j * t, t), t), :]
        w = jnp.exp(z - (spent + c))
        if past is not None:
            w = jnp.where(past, w, 0.0)
        return acc + _dot(w.astype(BF16), v)

    def sweep(tiles, c, acc):
        parts = [(j, past) + local(j, past) for j, past in tiles]
        for j, past, z, spent, total in parts:
            acc = finish(j, z, spent, c, acc, past)
            c = c + total
        return c, acc

    diagonal = [(qi * ratio + d, key + d * t < qry) for d in reversed(range(ratio))]
    carry = sweep(diagonal, jnp.zeros((tq, 1), F32), jnp.zeros((tq, HEAD_DIM), F32))

    def live(c):
        return jnp.min(c - z_bound) <= EXP_UNDERFLOW

    def cond(state):
        it, go, _, _ = state
        return jnp.logical_and(it < qi * (ratio // group), go)

    def body(state):
        it, _, c, acc = state
        first = qi * ratio - 1 - it * group
        c, acc = sweep([(first - g, None) for g in range(group)], c, acc)
        return it + 1, live(c), c, acc

    _, _, _, acc = lax.while_loop(cond, body, (jnp.int32(0), live(carry[0])) + carry)
    o_ref[...] = _head_rms(acc, g_ref[...]).astype(o_ref.dtype)


def _sb_attention(qkv, u, gain, *, q_col, k_col, v_col, n_heads):
    b, s, _ = qkv.shape
    t = ATT_TILE
    tq = SB_QUERY_TILE
    assert s % tq == 0 and tq % t == 0 and (tq // t) % SB_GROUP == 0
    return pl.pallas_call(
        functools.partial(_sb_kernel, tq=tq, t=t, group=SB_GROUP),
        grid=(b, n_heads, s // tq),
        in_specs=[pl.BlockSpec((None, tq, HEAD_DIM), lambda b, h, i: (b, i, q_col + h)),
                  pl.BlockSpec((None, s, HEAD_DIM), lambda b, h, i: (b, 0, k_col + h)),
                  pl.BlockSpec((None, s, HEAD_DIM), lambda b, h, i: (b, 0, v_col + h)),
                  pl.BlockSpec((t, t), lambda b, h, i: (0, 0)),
                  pl.BlockSpec((1, HEAD_DIM), lambda b, h, i: (0, h))],
        out_specs=pl.BlockSpec((None, tq, HEAD_DIM), lambda b, h, i: (b, i, h)),
        out_shape=jax.ShapeDtypeStruct((b, s, n_heads * HEAD_DIM), BF16),
        scratch_shapes=[pltpu.VMEM((1, 1), F32)],
        compiler_params=_params(("arbitrary", "arbitrary", "arbitrary"), 32),
        name="sb_attention",
    )(qkv, qkv, qkv, u, gain)


def _moba_kernel(slope_ref, q_ref, k_ref, v_ref, avg_ref, dist_ref, g_ref, o_ref,
                 km_ref, vt_ref, bias_ref, sc_ref, *, tq, t, n_blk, group):
    h = pl.program_id(1)
    qi = pl.program_id(2)
    ratio = tq // t

    @pl.when(qi == 0)
    def _():
        km_ref[...] = _dot(avg_ref[...], k_ref[...])
        _store_transposed_values(v_ref, vt_ref, t)

    q = q_ref[...]
    slope = slope_ref[h]
    alibi = slope * dist_ref[...]
    key = lax.broadcasted_iota(jnp.int32, (t, tq), 0)
    qry = lax.broadcasted_iota(jnp.int32, (t, tq), 1)
    lane = lax.broadcasted_iota(jnp.int32, (1, tq), 1)
    sub = functools.reduce(jnp.add, [(lane >= d * t).astype(jnp.int32) for d in range(1, ratio)])
    own = qi * ratio + sub

    def tile(j):
        k = k_ref[pl.ds(pl.multiple_of(j * t, t), t), :]
        return _dot_nt(k, q) + alibi - slope * (qi * tq - j * t).astype(F32)

    km_hi, km_lo = _split_bf16(km_ref[...])
    gate = _dot_nt(km_hi, q) + _dot_nt(km_lo, q)
    blk = lax.broadcasted_iota(jnp.int32, (n_blk, tq), 0)
    valid = blk < own
    work = jnp.where(valid, gate, -jnp.inf)
    sel = jnp.zeros((n_blk, tq), jnp.bool_)
    for _ in range(MOBA_TOPK):
        m = jnp.max(work, axis=0, keepdims=True)
        idx = jnp.min(jnp.where(work == m, blk, n_blk), axis=0, keepdims=True)
        hit = blk == idx
        sel = jnp.logical_or(sel, hit)
        work = jnp.where(hit, -jnp.inf, work)
    bias_ref[...] = jnp.where(jnp.logical_and(sel, valid), 0.0, NEG_INF)

    scores = []
    for d in range(ratio):
        j = qi * ratio + d
        mask = jnp.where(sub > d, bias_ref[pl.ds(j, 1), :],
                         jnp.where(jnp.logical_and(sub == d, key + d * t <= qry), 0.0, NEG_INF))
        scores.append(tile(j) + mask)
    m0 = functools.reduce(jnp.maximum, [jnp.max(s, axis=0, keepdims=True) for s in scores])
    l0 = jnp.zeros((1, tq), F32)
    acc0 = jnp.zeros((HEAD_DIM, tq), F32)
    for d, s in enumerate(scores):
        p = jnp.exp2(s - m0)
        l0 = l0 + jnp.sum(p, axis=0, keepdims=True)
        acc0 = acc0 + _dot(vt_ref[qi * ratio + d], p.astype(BF16))

    n_groups = qi * (ratio // group)
    dead = jnp.where(n_groups > 0, 0.0, NEG_INF)

    def store_scores(grp, slot):
        for g in range(group):
            j = grp * group + g
            sc_ref[slot, g] = tile(j) + (bias_ref[pl.ds(j, 1), :] + dead)

    def fold(grp, slot, carry):
        m_run, l_run, acc = carry
        scores = [sc_ref[slot, g] for g in range(group)]
        m_new = functools.reduce(
            jnp.maximum, [m_run] + [jnp.max(s, axis=0, keepdims=True) for s in scores])
        a = jnp.exp2(m_run - m_new)
        l_run = a * l_run
        acc = a * acc
        for g, s in enumerate(scores):
            p = jnp.exp2(s - m_new)
            l_run = l_run + jnp.sum(p, axis=0, keepdims=True)
            acc = acc + _dot(vt_ref[grp * group + g], p.astype(BF16))
        return m_new, l_run, acc

    def body(it, carry):
        carry = fold(it, lax.rem(it, 2), carry)
        store_scores(it + 1, lax.rem(it + 1, 2))
        return carry

    last = jnp.maximum(n_groups - 1, 0)
    store_scores(0, 0)
    carry = lax.fori_loop(0, last, body, (m0, l0, acc0))
    _, l_run, acc = fold(last, lax.rem(last, 2), carry)
    o_ref[...] = _head_rms_t(acc / l_run, g_ref[...]).astype(o_ref.dtype)


def _moba_attention(qkv, slopes, avg, dist_t, gain, *, q_col, k_col, v_col, n_heads):
    b, s, _ = qkv.shape
    t = MOBA_BLOCK
    tq = MOBA_QUERY_TILE
    group = MOBA_GROUP
    n_blk = s // t
    assert s % tq == 0 and tq % t == 0 and (tq // t) % group == 0
    grid_spec = pltpu.PrefetchScalarGridSpec(
        num_scalar_prefetch=1,
        grid=(b, n_heads, s // tq),
        in_specs=[pl.BlockSpec((None, tq, HEAD_DIM), lambda b, h, i, sl: (b, i, q_col + h)),
                  pl.BlockSpec((None, s, HEAD_DIM), lambda b, h, i, sl: (b, 0, k_col + h)),
                  pl.BlockSpec((None, s, HEAD_DIM), lambda b, h, i, sl: (b, 0, v_col + h)),
                  pl.BlockSpec((n_blk, s), lambda b, h, i, sl: (0, 0)),
                  pl.BlockSpec((t, tq), lambda b, h, i, sl: (0, 0)),
                  pl.BlockSpec((1, HEAD_DIM), lambda b, h, i, sl: (0, h))],
        out_specs=pl.BlockSpec((None, tq, HEAD_DIM), lambda b, h, i, sl: (b, i, h)),
        scratch_shapes=[pltpu.VMEM((n_blk, HEAD_DIM), F32),
                        pltpu.VMEM((n_blk, HEAD_DIM, t), BF16),
                        pltpu.VMEM((n_blk, tq), F32),
                        pltpu.VMEM((2, group, t, tq), F32)])
    return pl.pallas_call(
        functools.partial(_moba_kernel, tq=tq, t=t, n_blk=n_blk, group=group),
        grid_spec=grid_spec,
        out_shape=jax.ShapeDtypeStruct((b, s, n_heads * HEAD_DIM), BF16),
        compiler_params=_params(("arbitrary", "arbitrary", "arbitrary"), 32),
        name="moba_attention",
    )(slopes, qkv, qkv, qkv, avg, dist_t, gain)


def _layer_norm(x, g, b):
    mu = jnp.mean(x, axis=-1, keepdims=True)
    xc = x - mu
    var = jnp.mean(xc * xc, axis=-1, keepdims=True)
    return xc * lax.rsqrt(var + LN_EPS) * g + b


def _oproj_router_kernel(ms_ref, mm_ref, x_ref, wo_ref, g_ref, b_ref, wr_ref, br_ref, tri_ref,
                         x1_ref, te_ref, gt_ref, rk_ref, cnt_ref, carry_ref, *, sb_width):
    i = pl.program_id(0)

    @pl.when(i == 0)
    def _():
        carry_ref[...] = jnp.zeros_like(carry_ref)

    y = _dot(ms_ref[...], wo_ref[:sb_width, :]) + _dot(mm_ref[...], wo_ref[sb_width:, :])
    x1 = _layer_norm(DEEPNORM_ALPHA * x_ref[...] + y, g_ref[...], b_ref[...])
    x1_ref[...] = x1

    x_hi, x_lo = _split_bf16(x1)
    w_hi, w_lo = _split_bf16(wr_ref[...])
    logits = _dot(x_hi, w_hi) + _dot(x_hi, w_lo) + _dot(x_lo, w_hi) + br_ref[...]

    n_tok, n_exp = logits.shape
    lane = lax.broadcasted_iota(jnp.int32, (n_tok, n_exp), 1)
    work = logits
    vals, hits = [], []
    for _ in range(TOP_K):
        m = jnp.max(work, axis=1, keepdims=True)
        idx = jnp.min(jnp.where(work == m, lane, n_exp), axis=1, keepdims=True)
        hit = lane == idx
        vals.append(m)
        hits.append(hit)
        work = jnp.where(hit, -jnp.inf, work)
    exps = [jnp.exp(v - vals[0]) for v in vals]
    denom = functools.reduce(jnp.add, exps)

    onehot = functools.reduce(jnp.add, [hit.astype(F32) for hit in hits])
    rank_excl = carry_ref[...] + _dot(tri_ref[...], onehot.astype(BF16))
    carry_ref[...] = carry_ref[...] + jnp.sum(onehot, axis=0, keepdims=True)
    cnt_ref[...] = carry_ref[...]

    te_ref[...] = jnp.concatenate(
        [jnp.sum(jnp.where(hit, lane, 0), axis=1, keepdims=True) for hit in hits], axis=1)
    gt_ref[...] = jnp.concatenate([e / denom for e in exps], axis=1)
    rk_ref[...] = jnp.concatenate(
        [jnp.sum(jnp.where(hit, rank_excl, 0.0), axis=1, keepdims=True) for hit in hits],
        axis=1).astype(jnp.int32)


def _oproj_router(mixed_sb, mixed_mb, x2d, w_out, ln_g, ln_b, w_router, b_router, tri):
    n_tok, d = x2d.shape
    sb_width = mixed_sb.shape[1]
    mb_width = mixed_mb.shape[1]
    n_exp = w_router.shape[1]
    t = TOKEN_TILE
    const = lambda i: (0, 0)
    tok = lambda i: (i, 0)
    return pl.pallas_call(
        functools.partial(_oproj_router_kernel, sb_width=sb_width),
        grid=(n_tok // t,),
        in_specs=[pl.BlockSpec((t, sb_width), tok),
                  pl.BlockSpec((t, mb_width), tok),
                  pl.BlockSpec((t, d), tok),
                  pl.BlockSpec((sb_width + mb_width, d), const),
                  pl.BlockSpec((1, d), const),
                  pl.BlockSpec((1, d), const),
                  pl.BlockSpec((d, n_exp), const),
                  pl.BlockSpec((1, n_exp), const),
                  pl.BlockSpec((t, t), const)],
        out_specs=[pl.BlockSpec((t, d), tok),
                   pl.BlockSpec((t, TOP_K), tok),
                   pl.BlockSpec((t, TOP_K), tok),
                   pl.BlockSpec((t, TOP_K), tok),
                   pl.BlockSpec((1, n_exp), const)],
        out_shape=[jax.ShapeDtypeStruct((n_tok, d), F32),
                   jax.ShapeDtypeStruct((n_tok, TOP_K), jnp.int32),
                   jax.ShapeDtypeStruct((n_tok, TOP_K), F32),
                   jax.ShapeDtypeStruct((n_tok, TOP_K), jnp.int32),
                   jax.ShapeDtypeStruct((1, n_exp), F32)],
        scratch_shapes=[pltpu.VMEM((1, n_exp), F32)],
        compiler_params=_params(("arbitrary",), 48),
        name="oproj_ln_router",
    )(mixed_sb, mixed_mb, x2d, w_out, ln_g, ln_b, w_router, b_router, tri)


def _issue_row_copies(n_rows, make_copy):
    def body(it, carry):
        for u in range(DMA_UNROLL):
            make_copy(it * DMA_UNROLL + u).start(priority=u % 2)
        return carry

    lax.fori_loop(0, n_rows // DMA_UNROLL, body, 0)


def _gathered_rows(step, n_steps, idx_hbm, src_hbm, idx_ref, rows_ref, idx_sems, row_sems,
                   count=None):
    n_rows = rows_ref.shape[1]

    def idx_copy(s):
        slot = lax.rem(s, 2)
        return pltpu.make_async_copy(idx_hbm.at[s], idx_ref.at[slot], idx_sems.at[slot])

    def issue(s):
        slot = lax.rem(s, 2)
        idx_copy(s).wait()
        _issue_row_copies(n_rows if count is None else count(s), lambda r: pltpu.make_async_copy(
            src_hbm.at[pl.ds(idx_ref[slot, r], 1)], rows_ref.at[slot, pl.ds(r, 1)],
            row_sems.at[slot]))

    @pl.when(step == 0)
    def _():
        idx_copy(0).start()
        issue(0)
        if n_steps > 1:
            idx_copy(1).start()

    @pl.when(step + 1 < n_steps)
    def _():
        issue(step + 1)

        @pl.when(step + 2 < n_steps)
        def _():
            idx_copy(step + 2).start()

    slot = lax.rem(step, 2)
    if count is None:
        pltpu.make_async_copy(src_hbm.at[pl.ds(0, n_rows)], rows_ref.at[slot],
                              row_sems.at[slot]).wait()
    else:
        n = pl.multiple_of(count(step), DMA_UNROLL)

        @pl.when(n > 0)
        def _():
            pltpu.make_async_copy(src_hbm.at[pl.ds(0, n)], rows_ref.at[slot, pl.ds(0, n)],
                                  row_sems.at[slot]).wait()
    return slot


def _gather_scratch(n_rows, width, dtype):
    return [pltpu.SMEM((2, n_rows), jnp.int32),
            pltpu.VMEM((2, n_rows, width), dtype),
            pltpu.SemaphoreType.DMA((2,)),
            pltpu.SemaphoreType.DMA((2,))]


def _sort_gather_kernel(count_ref, tok_hbm, x_hbm, o_ref, idx_ref, rows_ref, idx_sems, row_sems,
                        *, n_steps):
    b = pl.program_id(0)

    @pl.when(b == 0)
    def _():
        rows_ref[...] = jnp.zeros_like(rows_ref)

    slot = _gathered_rows(b, n_steps, tok_hbm, x_hbm, idx_ref, rows_ref, idx_sems, row_sems,
                          count=lambda s: count_ref[s])
    o_ref[...] = rows_ref[slot].astype(o_ref.dtype)


def _sort_gather(block_rows, row_tok, x1, n_blocks):
    n_tok, d = x1.shape
    rb = ROW_BLOCK
    grid_spec = pltpu.PrefetchScalarGridSpec(
        num_scalar_prefetch=1,
        grid=(n_blocks,),
        in_specs=[pl.BlockSpec(memory_space=pl.ANY), pl.BlockSpec(memory_space=pl.ANY)],
        out_specs=pl.BlockSpec((rb, d), lambda b, cnt: (b, 0)),
        scratch_shapes=_gather_scratch(rb, d, x1.dtype))
    return pl.pallas_call(
        functools.partial(_sort_gather_kernel, n_steps=n_blocks),
        grid_spec=grid_spec,
        out_shape=jax.ShapeDtypeStruct((n_blocks * rb, d), BF16),
        compiler_params=_params(("arbitrary",), 24),
        name="sort_gather",
    )(block_rows, row_tok, x1)


def _expert_changed(be_ref, b):
    prev = be_ref[jnp.maximum(b - 1, 0)]
    return jnp.logical_or(b == 0, be_ref[b] != prev)


def _expert_up_kernel(be_ref, nused_ref, xs_ref, w_ref, bias_ref, sel_ref, o_ref, wb_ref):
    b = pl.program_id(1)

    @pl.when(b >= nused_ref[0])
    def _():
        o_ref[...] = jnp.zeros_like(o_ref)

    @pl.when(b < nused_ref[0])
    def _():
        @pl.when(_expert_changed(be_ref, b))
        def _():
            wb_ref[...] = w_ref[...].astype(BF16)

        hdn = _dot(xs_ref[...], wb_ref[...]) + bias_ref[...]
        tn = hdn.shape[1]
        nxt = pltpu.roll(hdn, tn - 1, 1)
        glu = jnp.minimum(hdn, SWIGLU_LIMIT)
        lin = jnp.clip(nxt, -SWIGLU_LIMIT, SWIGLU_LIMIT)
        act = (glu * jax.nn.sigmoid(SWIGLU_ALPHA * glu) * (lin + 1.0)).astype(BF16)
        sel = sel_ref[...]
        o_ref[...] = jnp.concatenate(
            [_dot(act[:, c:c + MXU_DIM], sel) for c in range(0, tn, MXU_DIM)], axis=1).astype(o_ref.dtype)


def _expert_up(block_e, n_used, xs, w_up, b_up, sel, n_blocks):
    n_rows, d = xs.shape
    n_exp, _, f2 = w_up.shape
    rb = ROW_BLOCK
    tn = EXPERT_TN

    def used(b, nu):
        return jnp.minimum(b, nu[0] - 1)

    grid_spec = pltpu.PrefetchScalarGridSpec(
        num_scalar_prefetch=2,
        grid=(f2 // tn, n_blocks),
        in_specs=[pl.BlockSpec((rb, d), lambda n, b, be, nu: (used(b, nu), 0)),
                  pl.BlockSpec((None, d, tn), lambda n, b, be, nu: (be[used(b, nu)], 0, n)),
                  pl.BlockSpec((None, 1, tn), lambda n, b, be, nu: (be[used(b, nu)], 0, n)),
                  pl.BlockSpec((MXU_DIM, MXU_DIM // 2), lambda n, b, be, nu: (0, 0))],
        out_specs=pl.BlockSpec((rb, tn // 2), lambda n, b, be, nu: (b, n)),
        scratch_shapes=[pltpu.VMEM((d, tn), BF16)])
    return pl.pallas_call(
        _expert_up_kernel,
        grid_spec=grid_spec,
        out_shape=jax.ShapeDtypeStruct((n_rows, f2 // 2), BF16),
        compiler_params=_params(("arbitrary", "arbitrary"), 48),
        name="expert_up",
    )(block_e, n_used, xs, w_up, b_up, sel)


def _expert_down_kernel(be_ref, nused_ref, a_ref, w_ref, bias_ref, o_ref, wb_ref):
    b = pl.program_id(1)

    @pl.when(b >= nused_ref[0])
    def _():
        o_ref[...] = jnp.zeros_like(o_ref)

    @pl.when(b < nused_ref[0])
    def _():
        @pl.when(_expert_changed(be_ref, b))
        def _():
            wb_ref[...] = w_ref[...].astype(BF16)

        o_ref[...] = _dot(a_ref[...], wb_ref[...]) + bias_ref[...]


def _expert_down(block_e, n_used, act, w_down, b_down, n_blocks):
    n_exp, f, d = w_down.shape
    rb = ROW_BLOCK
    tn = EXPERT_TN

    def used(b, nu):
        return jnp.minimum(b, nu[0] - 1)

    grid_spec = pltpu.PrefetchScalarGridSpec(
        num_scalar_prefetch=2,
        grid=(d // tn, n_blocks),
        in_specs=[pl.BlockSpec((rb, f), lambda n, b, be, nu: (used(b, nu), 0)),
                  pl.BlockSpec((None, f, tn), lambda n, b, be, nu: (be[used(b, nu)], 0, n)),
                  pl.BlockSpec((None, 1, tn), lambda n, b, be, nu: (be[used(b, nu)], 0, n))],
        out_specs=pl.BlockSpec((rb, tn), lambda n, b, be, nu: (b, n)),
        scratch_shapes=[pltpu.VMEM((f, tn), BF16)])
    return pl.pallas_call(
        _expert_down_kernel,
        grid_spec=grid_spec,
        out_shape=jax.ShapeDtypeStruct((n_blocks * rb, d), F32),
        compiler_params=_params(("arbitrary", "arbitrary"), 48),
        name="expert_down",
    )(block_e, n_used, act, w_down, b_down)


def _combine_kernel(dest_hbm, y_hbm, gate_ref, x1_ref, p_ref, g_ref, b_ref, wg_ref, bg_ref, wp_ref,
                    o_ref, idx_ref, rows_ref, idx_sems, row_sems, *, t, n_steps):
    slot = _gathered_rows(pl.program_id(0), n_steps, dest_hbm, y_hbm,
                          idx_ref, rows_ref, idx_sems, row_sems)
    gates = gate_ref[...]
    f = functools.reduce(
        jnp.add, [gates[:, k:k + 1] * rows_ref[slot, pl.ds(k * t, t), :] for k in range(TOP_K)])
    x2 = _layer_norm(DEEPNORM_ALPHA * x1_ref[...] + f, g_ref[...], b_ref[...])
    gate = jax.nn.sigmoid(_dot(x2.astype(BF16), wg_ref[...]) + bg_ref[...])
    emb = _dot(p_ref[...].astype(BF16), wp_ref[...])
    o_ref[...] = x2 + gate * emb


def _combine(dest, y_sorted, gates, x1, p2d, ln_g, ln_b, w_gate, b_gate, w_ple):
    n_tok, d = x1.shape
    ple = p2d.shape[1]
    t = TOKEN_TILE
    const = lambda i: (0, 0)
    tok = lambda i: (i, 0)
    return pl.pallas_call(
        functools.partial(_combine_kernel, t=t, n_steps=n_tok // t),
        grid=(n_tok // t,),
        in_specs=[pl.BlockSpec(memory_space=pl.ANY),
                  pl.BlockSpec(memory_space=pl.ANY),
                  pl.BlockSpec((t, TOP_K), tok),
                  pl.BlockSpec((t, d), tok),
                  pl.BlockSpec((t, ple), tok),
                  pl.BlockSpec((1, d), const),
                  pl.BlockSpec((1, d), const),
                  pl.BlockSpec((d, d), const),
                  pl.BlockSpec((1, d), const),
                  pl.BlockSpec((ple, d), const)],
        out_specs=pl.BlockSpec((t, d), tok),
        out_shape=jax.ShapeDtypeStruct((n_tok, d), F32),
        scratch_shapes=_gather_scratch(TOP_K * t, d, y_sorted.dtype),
        compiler_params=_params(("arbitrary",), 56),
        name="combine_ln_ple",
    )(dest, y_sorted, gates, x1, p2d, ln_g, ln_b, w_gate, b_gate, w_ple)


def _routing_tables(top_e, rank, counts, n_blocks):
    n_tok = top_e.shape[0]
    rb = ROW_BLOCK
    counts = counts.reshape(-1).astype(jnp.int32)
    padded = (counts + rb - 1) // rb * rb
    pad_end = jnp.cumsum(padded)
    pad_start = pad_end - padded
    dest = pad_start[top_e] + rank
    tok = jnp.broadcast_to(jnp.arange(n_tok, dtype=jnp.int32)[:, None], dest.shape)
    spread = jnp.arange(n_blocks * rb, dtype=jnp.int32) % n_tok
    row_tok = spread.at[dest.reshape(-1)].set(tok.reshape(-1), unique_indices=True)
    block_start = jnp.arange(n_blocks, dtype=jnp.int32) * rb
    block_e = jnp.minimum(
        jnp.sum((pad_end[None, :] <= block_start[:, None]).astype(jnp.int32), axis=1),
        counts.shape[0] - 1)
    n_used = (pad_end[-1] // rb).astype(jnp.int32).reshape(1)
    block_rows = jnp.clip(counts[block_e] - (block_start - pad_start[block_e]), 0, rb)
    block_rows = (block_rows + DMA_UNROLL - 1) // DMA_UNROLL * DMA_UNROLL
    dest_tiles = dest.reshape(n_tok // TOKEN_TILE, TOKEN_TILE, TOP_K).transpose(0, 2, 1)
    return (dest, dest_tiles.reshape(n_tok // TOKEN_TILE, TOP_K * TOKEN_TILE),
            row_tok.reshape(n_blocks, rb), block_e, n_used, block_rows.astype(jnp.int32))


def _const_tables(seq):
    t = ATT_TILE
    r = np.arange(t)
    u = (r[:, None] > r[None, :]).astype(np.float32)
    dist_t = (np.arange(MOBA_BLOCK)[:, None]
              - np.arange(MOBA_QUERY_TILE)[None, :]).astype(np.float32)
    n_blk = seq // MOBA_BLOCK
    avg = np.zeros((n_blk, seq), np.float32)
    for n in range(n_blk):
        avg[n, n * MOBA_BLOCK:(n + 1) * MOBA_BLOCK] = 1.0 / MOBA_BLOCK
    tt = np.arange(TOKEN_TILE)
    tri = (tt[None, :] < tt[:, None]).astype(np.float32)
    sel = np.zeros((MXU_DIM, MXU_DIM // 2), np.float32)
    sel[2 * np.arange(MXU_DIM // 2), np.arange(MXU_DIM // 2)] = 1.0
    slopes = 2.0 ** (-8.0 * np.arange(1, N_HEADS_MOBA + 1) / N_HEADS_MOBA)
    return (jnp.asarray(u, BF16), jnp.asarray(dist_t, F32), jnp.asarray(avg, BF16),
            jnp.asarray(tri, BF16), jnp.asarray(sel, BF16),
            jnp.asarray(slopes * LOG2_E, F32))


def kernel(x, p, w_in, norm_sb, norm_moba, w_out, ln1_g, ln1_b, w_router, b_router, w_up, b_up,
           w_down, b_down, ln2_g, ln2_b, w_ple, w_ple_gate, b_ple_gate):
    batch, seq, d = x.shape
    n_tok = batch * seq
    sb_width = N_HEADS_SB * HEAD_DIM
    mb_width = N_HEADS_MOBA * HEAD_DIM
    n_exp = w_router.shape[-1]
    n_blocks = -(-n_tok * TOP_K // ROW_BLOCK) + n_exp
    u, dist_t, avg, tri, sel, slopes2 = _const_tables(seq)
    heads = sb_width // HEAD_DIM
    x2d = x.reshape(n_tok, d)
    q_scales = ((0, 1.0 / math.sqrt(HEAD_DIM)),
                (3 * sb_width // PROJ_TN, LOG2_E / math.sqrt(HEAD_DIM)))

    for i in range(DEPTH):
        qkv = _qkv_proj(x2d, w_in[i].astype(BF16), q_scales).reshape(batch, seq, -1)
        mixed_sb = _sb_attention(qkv, u, norm_sb[i].reshape(1, -1),
                                 q_col=0, k_col=heads, v_col=2 * heads, n_heads=N_HEADS_SB)
        mixed_mb = _moba_attention(qkv, slopes2, avg, dist_t, norm_moba[i].reshape(1, -1),
                                   q_col=3 * heads, k_col=3 * heads + N_HEADS_MOBA,
                                   v_col=3 * heads + 2 * N_HEADS_MOBA, n_heads=N_HEADS_MOBA)
        x1, top_e, gates, rank, counts = _oproj_router(
            mixed_sb.reshape(n_tok, sb_width), mixed_mb.reshape(n_tok, mb_width), x2d,
            w_out[i].astype(BF16), ln1_g[i].reshape(1, -1), ln1_b[i].reshape(1, -1),
            w_router[i], b_router[i].reshape(1, -1), tri)
        _, dest_tiles, row_tok, block_e, n_used, block_rows = _routing_tables(
            top_e, rank, counts, n_blocks)
        xs = _sort_gather(block_rows, row_tok, x1, n_blocks)
        act = _expert_up(block_e, n_used, xs, w_up[i], b_up[i].reshape(n_exp, 1, -1), sel, n_blocks)
        y_sorted = _expert_down(block_e, n_used, act, w_down[i], b_down[i].reshape(n_exp, 1, -1),
                                n_blocks)
        x2d = _combine(dest_tiles, y_sorted, gates, x1,
                       p[i].reshape(n_tok, -1), ln2_g[i].reshape(1, -1), ln2_b[i].reshape(1, -1),
                       w_ple_gate[i].astype(BF16), b_ple_gate[i].reshape(1, -1),
                       w_ple[i].astype(BF16))
    return x2d.reshape(batch, seq, d)
```

```python
import functools
import math

import jax
import jax.numpy as jnp
import numpy as np
from jax import lax
from jax.experimental import pallas as pl
from jax.experimental.pallas import tpu as pltpu

HEAD_DIM = 128
N_HEADS_SB = 8
N_HEADS_MOBA = 8
MOBA_BLOCK = 256
MOBA_TOPK = 3
N_EXPERTS = 32
TOP_K = 4
SWIGLU_LIMIT = 7.0
SWIGLU_ALPHA = 1.702
LN_EPS = 1e-5
RMS_EPS = 1e-6
NEG_INF = -1e30
DEPTH = 1
DEEPNORM_ALPHA = (2.0 * DEPTH) ** 0.25
LOG2_E = math.log2(math.e)
EXP_UNDERFLOW = 110.0
EXP2_UNDERFLOW = 160.0

ATT_TILE = 256
MOBA_QUERY_TILE = 512
MOBA_GROUP = 2
SB_QUERY_TILE = 512
SB_GROUP = 2
ROW_BLOCK = 512
TOKEN_TILE = 256
PROJ_TM = 1024
PROJ_TN = 1024
EXPERT_TN = 1024
MXU_DIM = 256
DMA_UNROLL = 8
MIB = 1024 * 1024

F32 = jnp.float32
BF16 = jnp.bfloat16


def _params(semantics, vmem_mib):
    return pltpu.CompilerParams(dimension_semantics=semantics,
                                vmem_limit_bytes=vmem_mib * MIB)


def _split_bf16(a):
    hi = a.astype(BF16)
    lo = (a - hi.astype(F32)).astype(BF16)
    return hi, lo


def _dot(a, b):
    return jnp.dot(a, b, preferred_element_type=F32)


def _dot_nt(a, b):
    return lax.dot_general(a, b, (((1,), (1,)), ((), ())), preferred_element_type=F32)


def _qkv_kernel(x_ref, w_ref, o_ref, xb_ref, *, tile_scales):
    n = pl.program_id(1)

    @pl.when(n == 0)
    def _():
        xb_ref[...] = x_ref[...].astype(BF16)

    scale = jnp.float32(1.0)
    for tile, s in tile_scales:
        scale = jnp.where(n == tile, s, scale)
    o_ref[...] = (_dot(xb_ref[...], w_ref[...]) * scale).astype(o_ref.dtype)


def _qkv_proj(x2d, w_bf16, tile_scales):
    n_tok, d = x2d.shape
    n_out = w_bf16.shape[1]
    tm = min(PROJ_TM, n_tok)
    tn = PROJ_TN
    return pl.pallas_call(
        functools.partial(_qkv_kernel, tile_scales=tile_scales),
        grid=(n_tok // tm, n_out // tn),
        in_specs=[pl.BlockSpec((tm, d), lambda m, n: (m, 0)),
                  pl.BlockSpec((d, tn), lambda m, n: (0, n))],
        out_specs=pl.BlockSpec((tm, tn), lambda m, n: (m, n)),
        out_shape=jax.ShapeDtypeStruct((n_tok, n_out), BF16),
        scratch_shapes=[pltpu.VMEM((tm, d), BF16)],
        compiler_params=_params(("arbitrary", "arbitrary"), 48),
        name="qkv_proj",
    )(x2d, w_bf16)


def _head_rms(o, gain):
    ms = jnp.mean(o * o, axis=-1, keepdims=True)
    return o * lax.rsqrt(ms + RMS_EPS) * gain


def _head_rms_t(o_t, gain):
    ms = jnp.mean(o_t * o_t, axis=0, keepdims=True)
    return (o_t * lax.rsqrt(ms + RMS_EPS)).T * gain


def _store_transposed_values(v_ref, vt_ref, t):
    def body(n, carry):
        v = v_ref[pl.ds(pl.multiple_of(n * t, t), t), :]
        vt_ref[n] = v.astype(F32).T.astype(vt_ref.dtype)
        return carry

    lax.fori_loop(0, vt_ref.shape[0], body, 0)


def _sb_kernel(q_ref, k_ref, v_ref, u_ref, g_ref, o_ref, knorm_ref, *, tq, t, group):
    qi = pl.program_id(2)
    ratio = tq // t

    @pl.when(qi == 0)
    def _():
        def body(n, m):
            k = k_ref[pl.ds(pl.multiple_of(n * t, t), t), :].astype(F32)
            return jnp.maximum(m, jnp.max(jnp.sum(k * k, axis=1, keepdims=True), axis=0, keepdims=True))

        knorm_ref[...] = jnp.sqrt(lax.fori_loop(0, k_ref.shape[0] // t, body, jnp.zeros((1, 1), F32)))

    q = q_ref[...]
    u = u_ref[...]
    qry = lax.broadcasted_iota(jnp.int32, (tq, t), 0)
    key = lax.broadcasted_iota(jnp.int32, (tq, t), 1)
    qf = q.astype(F32)
    z_bound = jnp.sqrt(jnp.sum(qf * qf, axis=1, keepdims=True)) * knorm_ref[...] * 1.001 + 1e-3

    def local(j, past):
        k = k_ref[pl.ds(pl.multiple_of(j * t, t), t), :]
        z = _dot_nt(q, k)
        drop = jnp.maximum(z, 0.0) + jnp.log(1.0 + jnp.exp(-jnp.abs(z)))
        if past is not None:
            drop = jnp.where(past, drop, 0.0)
        hi, lo = _split_bf16(drop)
        between = _dot(hi, u) + _dot(lo, u)
        first = slice(0, HEAD_DIM)
        total = (between[:, first] + drop[:, first])[:, :1]
        return z, drop + between, total

    def finish(j, z, spent, c, acc, past):
        v = v_ref[pl.ds(pl.multiple_of(j * t, t), t), :]
        w = jnp.exp(z - (spent + c))
        if past is not None:
            w = jnp.where(past, w, 0.0)
        return acc + _dot(w.astype(BF16), v)

    def sweep(tiles, c, acc):
        parts = [(j, past) + local(j, past) for j, past in tiles]
        for j, past, z, spent, total in parts:
            acc = finish(j, z, spent, c, acc, past)
            c = c + total
        return c, acc

    diagonal = [(qi * ratio + d, key + d * t < qry) for d in reversed(range(ratio))]
    carry = sweep(diagonal, jnp.zeros((tq, 1), F32), jnp.zeros((tq, HEAD_DIM), F32))

    def live(c):
        return jnp.min(c - z_bound) <= EXP_UNDERFLOW

    def cond(state):
        it, go, _, _ = state
        return jnp.logical_and(it < qi * (ratio // group), go)

    def body(state):
        it, _, c, acc = state
        first = qi * ratio - 1 - it * group
        c, acc = sweep([(first - g, None) for g in range(group)], c, acc)
        return it + 1, live(c), c, acc

    _, _, _, acc = lax.while_loop(cond, body, (jnp.int32(0), live(carry[0])) + carry)
    o_ref[...] = _head_rms(acc, g_ref[...]).astype(o_ref.dtype)


def _sb_attention(qkv, u, gain, *, q_col, k_col, v_col, n_heads):
    b, s, _ = qkv.shape
    t = ATT_TILE
    tq = SB_QUERY_TILE
    assert s % tq == 0 and tq % t == 0 and (tq // t) % SB_GROUP == 0
    return pl.pallas_call(
        functools.partial(_sb_kernel, tq=tq, t=t, group=SB_GROUP),
        grid=(b, n_heads, s // tq),
        in_specs=[pl.BlockSpec((None, tq, HEAD_DIM), lambda b, h, i: (b, i, q_col + h)),
                  pl.BlockSpec((None, s, HEAD_DIM), lambda b, h, i: (b, 0, k_col + h)),
                  pl.BlockSpec((None, s, HEAD_DIM), lambda b, h, i: (b, 0, v_col + h)),
                  pl.BlockSpec((t, t), lambda b, h, i: (0, 0)),
                  pl.BlockSpec((1, HEAD_DIM), lambda b, h, i: (0, h))],
        out_specs=pl.BlockSpec((None, tq, HEAD_DIM), lambda b, h, i: (b, i, h)),
        out_shape=jax.ShapeDtypeStruct((b, s, n_heads * HEAD_DIM), BF16),
        scratch_shapes=[pltpu.VMEM((1, 1), F32)],
        compiler_params=_params(("arbitrary", "arbitrary", "arbitrary"), 32),
        name="sb_attention",
    )(qkv, qkv, qkv, u, gain)


def _moba_kernel(slope_ref, q_ref, k_ref, v_ref, avg_ref, dist_ref, g_ref, o_ref,
                 km_ref, vt_ref, bias_ref, sc_ref, knorm_ref, *, tq, t, n_blk, group):
    h = pl.program_id(1)
    qi = pl.program_id(2)
    ratio = tq // t

    @pl.when(qi == 0)
    def _():
        km_ref[...] = _dot(avg_ref[...], k_ref[...])
        _store_transposed_values(v_ref, vt_ref, t)

        def body(n, m):
            k = k_ref[pl.ds(pl.multiple_of(n * t, t), t), :].astype(F32)
            return jnp.maximum(m, jnp.max(jnp.sum(k * k, axis=1, keepdims=True), axis=0, keepdims=True))

        knorm_ref[...] = jnp.sqrt(lax.fori_loop(0, n_blk, body, jnp.zeros((1, 1), F32)))

    q = q_ref[...]
    slope = slope_ref[h]
    n_heads = slope_ref.shape[0] // 2
    inv_slope = slope_ref[n_heads + h]
    alibi = slope * dist_ref[...]
    key = lax.broadcasted_iota(jnp.int32, (t, tq), 0)
    qry = lax.broadcasted_iota(jnp.int32, (t, tq), 1)
    lane = lax.broadcasted_iota(jnp.int32, (1, tq), 1)
    sub = functools.reduce(jnp.add, [(lane >= d * t).astype(jnp.int32) for d in range(1, ratio)])
    own = qi * ratio + sub

    def tile(j):
        k = k_ref[pl.ds(pl.multiple_of(j * t, t), t), :]
        return _dot_nt(k, q) + alibi - slope * (qi * tq - j * t).astype(F32)

    km_hi, km_lo = _split_bf16(km_ref[...])
    gate = _dot_nt(km_hi, q) + _dot_nt(km_lo, q)
    blk = lax.broadcasted_iota(jnp.int32, (n_blk, tq), 0)
    valid = blk < own
    work = jnp.where(valid, gate, -jnp.inf)
    sel = jnp.zeros((n_blk, tq), jnp.bool_)
    for _ in range(MOBA_TOPK):
        m = jnp.max(work, axis=0, keepdims=True)
        idx = jnp.min(jnp.where(work == m, blk, n_blk), axis=0, keepdims=True)
        hit = blk == idx
        sel = jnp.logical_or(sel, hit)
        work = jnp.where(hit, -jnp.inf, work)
    bias_ref[...] = jnp.where(jnp.logical_and(sel, valid), 0.0, NEG_INF)

    scores = []
    for d in range(ratio):
        j = qi * ratio + d
        mask = jnp.where(sub > d, bias_ref[pl.ds(j, 1), :],
                         jnp.where(jnp.logical_and(sub == d, key + d * t <= qry), 0.0, NEG_INF))
        scores.append(tile(j) + mask)
    m0 = functools.reduce(jnp.maximum, [jnp.max(s, axis=0, keepdims=True) for s in scores])
    l0 = jnp.zeros((1, tq), F32)
    acc0 = jnp.zeros((HEAD_DIM, tq), F32)
    for d, s in enumerate(scores):
        p = jnp.exp2(s - m0)
        l0 = l0 + jnp.sum(p, axis=0, keepdims=True)
        acc0 = acc0 + _dot(vt_ref[qi * ratio + d], p.astype(BF16))

    n_groups = qi * (ratio // group)
    dead = jnp.where(n_groups > 0, 0.0, NEG_INF)

    def store_scores(grp, slot):
        for g in range(group):
            j = grp * group + g
            sc_ref[slot, g] = tile(j) + (bias_ref[pl.ds(j, 1), :] + dead)

    def fold(grp, slot, carry):
        m_run, l_run, acc = carry
        scores = [sc_ref[slot, g] for g in range(group)]
        m_new = functools.reduce(
            jnp.maximum, [m_run] + [jnp.max(s, axis=0, keepdims=True) for s in scores])
        a = jnp.exp2(m_run - m_new)
        l_run = a * l_run
        acc = a * acc
        for g, s in enumerate(scores):
            p = jnp.exp2(s - m_new)
            l_run = l_run + jnp.sum(p, axis=0, keepdims=True)
            acc = acc + _dot(vt_ref[grp * group + g], p.astype(BF16))
        return m_new, l_run, acc

    def body(it, carry):
        carry = fold(it, lax.rem(it, 2), carry)
        store_scores(it + 1, lax.rem(it + 1, 2))
        return carry

    qf = q.astype(F32)
    z_bound = (jnp.sqrt(jnp.max(jnp.sum(qf * qf, axis=1, keepdims=True), axis=0, keepdims=True))
               * knorm_ref[...] * 1.001 + 1e-3)
    reach = (EXP2_UNDERFLOW + 2.0 * z_bound) * inv_slope
    skip = jnp.floor(((qi * tq).astype(F32) - reach) * (1.0 / (t * group))).astype(jnp.int32)
    last = jnp.maximum(n_groups - 1, 0)
    first = jnp.clip(skip[0, 0], 0, last)
    store_scores(first, lax.rem(first, 2))
    carry = lax.fori_loop(first, last, body, (m0, l0, acc0))
    _, l_run, acc = fold(last, lax.rem(last, 2), carry)
    o_ref[...] = _head_rms_t(acc / l_run, g_ref[...]).astype(o_ref.dtype)


def _moba_attention(qkv, slopes, avg, dist_t, gain, *, q_col, k_col, v_col, n_heads):
    b, s, _ = qkv.shape
    t = MOBA_BLOCK
    tq = MOBA_QUERY_TILE
    group = MOBA_GROUP
    n_blk = s // t
    assert s % tq == 0 and tq % t == 0 and (tq // t) % group == 0
    grid_spec = pltpu.PrefetchScalarGridSpec(
        num_scalar_prefetch=1,
        grid=(b, n_heads, s // tq),
        in_specs=[pl.BlockSpec((None, tq, HEAD_DIM), lambda b, h, i, sl: (b, i, q_col + h)),
                  pl.BlockSpec((None, s, HEAD_DIM), lambda b, h, i, sl: (b, 0, k_col + h)),
                  pl.BlockSpec((None, s, HEAD_DIM), lambda b, h, i, sl: (b, 0, v_col + h)),
                  pl.BlockSpec((n_blk, s), lambda b, h, i, sl: (0, 0)),
                  pl.BlockSpec((t, tq), lambda b, h, i, sl: (0, 0)),
                  pl.BlockSpec((1, HEAD_DIM), lambda b, h, i, sl: (0, h))],
        out_specs=pl.BlockSpec((None, tq, HEAD_DIM), lambda b, h, i, sl: (b, i, h)),
        scratch_shapes=[pltpu.VMEM((n_blk, HEAD_DIM), F32),
                        pltpu.VMEM((n_blk, HEAD_DIM, t), BF16),
                        pltpu.VMEM((n_blk, tq), F32),
                        pltpu.VMEM((2, group, t, tq), F32),
                        pltpu.VMEM((1, 1), F32)])
    return pl.pallas_call(
        functools.partial(_moba_kernel, tq=tq, t=t, n_blk=n_blk, group=group),
        grid_spec=grid_spec,
        out_shape=jax.ShapeDtypeStruct((b, s, n_heads * HEAD_DIM), BF16),
        compiler_params=_params(("arbitrary", "arbitrary", "arbitrary"), 32),
        name="moba_attention",
    )(slopes, qkv, qkv, qkv, avg, dist_t, gain)


def _layer_norm(x, g, b):
    mu = jnp.mean(x, axis=-1, keepdims=True)
    xc = x - mu
    var = jnp.mean(xc * xc, axis=-1, keepdims=True)
    return xc * lax.rsqrt(var + LN_EPS) * g + b


def _oproj_router_kernel(ms_ref, mm_ref, x_ref, wo_ref, g_ref, b_ref, wr_ref, br_ref, tri_ref,
                         x1_ref, te_ref, gt_ref, rk_ref, cnt_ref, carry_ref, *, sb_width):
    i = pl.program_id(0)

    @pl.when(i == 0)
    def _():
        carry_ref[...] = jnp.zeros_like(carry_ref)

    y = _dot(ms_ref[...], wo_ref[:sb_width, :]) + _dot(mm_ref[...], wo_ref[sb_width:, :])
    x1 = _layer_norm(DEEPNORM_ALPHA * x_ref[...] + y, g_ref[...], b_ref[...])
    x1_ref[...] = x1

    x_hi, x_lo = _split_bf16(x1)
    w_hi, w_lo = _split_bf16(wr_ref[...])
    logits = _dot(x_hi, w_hi) + _dot(x_hi, w_lo) + _dot(x_lo, w_hi) + br_ref[...]

    n_tok, n_exp = logits.shape
    lane = lax.broadcasted_iota(jnp.int32, (n_tok, n_exp), 1)
    work = logits
    vals, hits = [], []
    for _ in range(TOP_K):
        m = jnp.max(work, axis=1, keepdims=True)
        idx = jnp.min(jnp.where(work == m, lane, n_exp), axis=1, keepdims=True)
        hit = lane == idx
        vals.append(m)
        hits.append(hit)
        work = jnp.where(hit, -jnp.inf, work)
    exps = [jnp.exp(v - vals[0]) for v in vals]
    denom = functools.reduce(jnp.add, exps)

    onehot = functools.reduce(jnp.add, [hit.astype(F32) for hit in hits])
    rank_excl = carry_ref[...] + _dot(tri_ref[...], onehot.astype(BF16))
    carry_ref[...] = carry_ref[...] + jnp.sum(onehot, axis=0, keepdims=True)
    cnt_ref[...] = carry_ref[...]

    te_ref[...] = jnp.concatenate(
        [jnp.sum(jnp.where(hit, lane, 0), axis=1, keepdims=True) for hit in hits], axis=1)
    gt_ref[...] = jnp.concatenate([e / denom for e in exps], axis=1)
    rk_ref[...] = jnp.concatenate(
        [jnp.sum(jnp.where(hit, rank_excl, 0.0), axis=1, keepdims=True) for hit in hits],
        axis=1).astype(jnp.int32)


def _oproj_router(mixed_sb, mixed_mb, x2d, w_out, ln_g, ln_b, w_router, b_router, tri):
    n_tok, d = x2d.shape
    sb_width = mixed_sb.shape[1]
    mb_width = mixed_mb.shape[1]
    n_exp = w_router.shape[1]
    t = TOKEN_TILE
    const = lambda i: (0, 0)
    tok = lambda i: (i, 0)
    return pl.pallas_call(
        functools.partial(_oproj_router_kernel, sb_width=sb_width),
        grid=(n_tok // t,),
        in_specs=[pl.BlockSpec((t, sb_width), tok),
                  pl.BlockSpec((t, mb_width), tok),
                  pl.BlockSpec((t, d), tok),
                  pl.BlockSpec((sb_width + mb_width, d), const),
                  pl.BlockSpec((1, d), const),
                  pl.BlockSpec((1, d), const),
                  pl.BlockSpec((d, n_exp), const),
                  pl.BlockSpec((1, n_exp), const),
                  pl.BlockSpec((t, t), const)],
        out_specs=[pl.BlockSpec((t, d), tok),
                   pl.BlockSpec((t, TOP_K), tok),
                   pl.BlockSpec((t, TOP_K), tok),
                   pl.BlockSpec((t, TOP_K), tok),
                   pl.BlockSpec((1, n_exp), const)],
        out_shape=[jax.ShapeDtypeStruct((n_tok, d), F32),
                   jax.ShapeDtypeStruct((n_tok, TOP_K), jnp.int32),
                   jax.ShapeDtypeStruct((n_tok, TOP_K), F32),
                   jax.ShapeDtypeStruct((n_tok, TOP_K), jnp.int32),
                   jax.ShapeDtypeStruct((1, n_exp), F32)],
        scratch_shapes=[pltpu.VMEM((1, n_exp), F32)],
        compiler_params=_params(("arbitrary",), 48),
        name="oproj_ln_router",
    )(mixed_sb, mixed_mb, x2d, w_out, ln_g, ln_b, w_router, b_router, tri)


def _issue_row_copies(n_rows, make_copy):
    def body(it, carry):
        for u in range(DMA_UNROLL):
            make_copy(it * DMA_UNROLL + u).start(priority=u % 2)
        return carry

    lax.fori_loop(0, n_rows // DMA_UNROLL, body, 0)


def _gathered_rows(step, n_steps, idx_hbm, src_hbm, idx_ref, rows_ref, idx_sems, row_sems,
                   count=None):
    n_rows = rows_ref.shape[1]

    def idx_copy(s):
        slot = lax.rem(s, 2)
        return pltpu.make_async_copy(idx_hbm.at[s], idx_ref.at[slot], idx_sems.at[slot])

    def issue(s):
        slot = lax.rem(s, 2)
        idx_copy(s).wait()
        _issue_row_copies(n_rows if count is None else count(s), lambda r: pltpu.make_async_copy(
            src_hbm.at[pl.ds(idx_ref[slot, r], 1)], rows_ref.at[slot, pl.ds(r, 1)],
            row_sems.at[slot]))

    @pl.when(step == 0)
    def _():
        idx_copy(0).start()
        issue(0)
        if n_steps > 1:
            idx_copy(1).start()

    @pl.when(step + 1 < n_steps)
    def _():
        issue(step + 1)

        @pl.when(step + 2 < n_steps)
        def _():
            idx_copy(step + 2).start()

    slot = lax.rem(step, 2)
    if count is None:
        pltpu.make_async_copy(src_hbm.at[pl.ds(0, n_rows)], rows_ref.at[slot],
                              row_sems.at[slot]).wait()
    else:
        n = pl.multiple_of(count(step), DMA_UNROLL)

        @pl.when(n > 0)
        def _():
            pltpu.make_async_copy(src_hbm.at[pl.ds(0, n)], rows_ref.at[slot, pl.ds(0, n)],
                                  row_sems.at[slot]).wait()
    return slot


def _gather_scratch(n_rows, width, dtype):
    return [pltpu.SMEM((2, n_rows), jnp.int32),
            pltpu.VMEM((2, n_rows, width), dtype),
            pltpu.SemaphoreType.DMA((2,)),
            pltpu.SemaphoreType.DMA((2,))]


def _sort_gather_kernel(count_ref, tok_hbm, x_hbm, o_ref, idx_ref, rows_ref, idx_sems, row_sems,
                        *, n_steps):
    b = pl.program_id(0)

    @pl.when(b == 0)
    def _():
        rows_ref[...] = jnp.zeros_like(rows_ref)

    slot = _gathered_rows(b, n_steps, tok_hbm, x_hbm, idx_ref, rows_ref, idx_sems, row_sems,
                          count=lambda s: count_ref[s])
    o_ref[...] = rows_ref[slot].astype(o_ref.dtype)


def _sort_gather(block_rows, row_tok, x1, n_blocks):
    n_tok, d = x1.shape
    rb = ROW_BLOCK
    grid_spec = pltpu.PrefetchScalarGridSpec(
        num_scalar_prefetch=1,
        grid=(n_blocks,),
        in_specs=[pl.BlockSpec(memory_space=pl.ANY), pl.BlockSpec(memory_space=pl.ANY)],
        out_specs=pl.BlockSpec((rb, d), lambda b, cnt: (b, 0)),
        scratch_shapes=_gather_scratch(rb, d, x1.dtype))
    return pl.pallas_call(
        functools.partial(_sort_gather_kernel, n_steps=n_blocks),
        grid_spec=grid_spec,
        out_shape=jax.ShapeDtypeStruct((n_blocks * rb, d), BF16),
        compiler_params=_params(("arbitrary",), 24),
        name="sort_gather",
    )(block_rows, row_tok, x1)


def _expert_changed(be_ref, b):
    prev = be_ref[jnp.maximum(b - 1, 0)]
    return jnp.logical_or(b == 0, be_ref[b] != prev)


def _expert_up_kernel(be_ref, nused_ref, xs_ref, w_ref, bias_ref, sel_ref, o_ref, wb_ref):
    b = pl.program_id(1)

    @pl.when(b >= nused_ref[0])
    def _():
        o_ref[...] = jnp.zeros_like(o_ref)

    @pl.when(b < nused_ref[0])
    def _():
        @pl.when(_expert_changed(be_ref, b))
        def _():
            wb_ref[...] = w_ref[...].astype(BF16)

        hdn = _dot(xs_ref[...], wb_ref[...]) + bias_ref[...]
        tn = hdn.shape[1]
        nxt = pltpu.roll(hdn, tn - 1, 1)
        glu = jnp.minimum(hdn, SWIGLU_LIMIT)
        lin = jnp.clip(nxt, -SWIGLU_LIMIT, SWIGLU_LIMIT)
        act = (glu * jax.nn.sigmoid(SWIGLU_ALPHA * glu) * (lin + 1.0)).astype(BF16)
        sel = sel_ref[...]
        o_ref[...] = jnp.concatenate(
            [_dot(act[:, c:c + MXU_DIM], sel) for c in range(0, tn, MXU_DIM)], axis=1).astype(o_ref.dtype)


def _expert_up(block_e, n_used, xs, w_up, b_up, sel, n_blocks):
    n_rows, d = xs.shape
    n_exp, _, f2 = w_up.shape
    rb = ROW_BLOCK
    tn = EXPERT_TN

    def used(b, nu):
        return jnp.minimum(b, nu[0] - 1)

    grid_spec = pltpu.PrefetchScalarGridSpec(
        num_scalar_prefetch=2,
        grid=(f2 // tn, n_blocks),
        in_specs=[pl.BlockSpec((rb, d), lambda n, b, be, nu: (used(b, nu), 0)),
                  pl.BlockSpec((None, d, tn), lambda n, b, be, nu: (be[used(b, nu)], 0, n)),
                  pl.BlockSpec((None, 1, tn), lambda n, b, be, nu: (be[used(b, nu)], 0, n)),
                  pl.BlockSpec((MXU_DIM, MXU_DIM // 2), lambda n, b, be, nu: (0, 0))],
        out_specs=pl.BlockSpec((rb, tn // 2), lambda n, b, be, nu: (b, n)),
        scratch_shapes=[pltpu.VMEM((d, tn), BF16)])
    return pl.pallas_call(
        _expert_up_kernel,
        grid_spec=grid_spec,
        out_shape=jax.ShapeDtypeStruct((n_rows, f2 // 2), BF16),
        compiler_params=_params(("arbitrary", "arbitrary"), 48),
        name="expert_up",
    )(block_e, n_used, xs, w_up, b_up, sel)


def _expert_down_kernel(be_ref, nused_ref, a_ref, w_ref, bias_ref, o_ref, wb_ref):
    b = pl.program_id(1)

    @pl.when(b >= nused_ref[0])
    def _():
        o_ref[...] = jnp.zeros_like(o_ref)

    @pl.when(b < nused_ref[0])
    def _():
        @pl.when(_expert_changed(be_ref, b))
        def _():
            wb_ref[...] = w_ref[...].astype(BF16)

        o_ref[...] = _dot(a_ref[...], wb_ref[...]) + bias_ref[...]


def _expert_down(block_e, n_used, act, w_down, b_down, n_blocks):
    n_exp, f, d = w_down.shape
    rb = ROW_BLOCK
    tn = EXPERT_TN

    def used(b, nu):
        return jnp.minimum(b, nu[0] - 1)

    grid_spec = pltpu.PrefetchScalarGridSpec(
        num_scalar_prefetch=2,
        grid=(d // tn, n_blocks),
        in_specs=[pl.BlockSpec((rb, f), lambda n, b, be, nu: (used(b, nu), 0)),
                  pl.BlockSpec((None, f, tn), lambda n, b, be, nu: (be[used(b, nu)], 0, n)),
                  pl.BlockSpec((None, 1, tn), lambda n, b, be, nu: (be[used(b, nu)], 0, n))],
        out_specs=pl.BlockSpec((rb, tn), lambda n, b, be, nu: (b, n)),
        scratch_shapes=[pltpu.VMEM((f, tn), BF16)])
    return pl.pallas_call(
        _expert_down_kernel,
        grid_spec=grid_spec,
        out_shape=jax.ShapeDtypeStruct((n_blocks * rb, d), F32),
        compiler_params=_params(("arbitrary", "arbitrary"), 48),
        name="expert_down",
    )(block_e, n_used, act, w_down, b_down)


def _combine_kernel(dest_hbm, y_hbm, gate_ref, x1_ref, p_ref, g_ref, b_ref, wg_ref, bg_ref, wp_ref,
                    o_ref, idx_ref, rows_ref, idx_sems, row_sems, *, t, n_steps):
    slot = _gathered_rows(pl.program_id(0), n_steps, dest_hbm, y_hbm,
                          idx_ref, rows_ref, idx_sems, row_sems)
    gates = gate_ref[...]
    f = functools.reduce(
        jnp.add, [gates[:, k:k + 1] * rows_ref[slot, pl.ds(k * t, t), :] for k in range(TOP_K)])
    x2 = _layer_norm(DEEPNORM_ALPHA * x1_ref[...] + f, g_ref[...], b_ref[...])
    gate = jax.nn.sigmoid(_dot(x2.astype(BF16), wg_ref[...]) + bg_ref[...])
    emb = _dot(p_ref[...].astype(BF16), wp_ref[...])
    o_ref[...] = x2 + gate * emb


def _combine(dest, y_sorted, gates, x1, p2d, ln_g, ln_b, w_gate, b_gate, w_ple):
    n_tok, d = x1.shape
    ple = p2d.shape[1]
    t = TOKEN_TILE
    const = lambda i: (0, 0)
    tok = lambda i: (i, 0)
    return pl.pallas_call(
        functools.partial(_combine_kernel, t=t, n_steps=n_tok // t),
        grid=(n_tok // t,),
        in_specs=[pl.BlockSpec(memory_space=pl.ANY),
                  pl.BlockSpec(memory_space=pl.ANY),
                  pl.BlockSpec((t, TOP_K), tok),
                  pl.BlockSpec((t, d), tok),
                  pl.BlockSpec((t, ple), tok),
                  pl.BlockSpec((1, d), const),
                  pl.BlockSpec((1, d), const),
                  pl.BlockSpec((d, d), const),
                  pl.BlockSpec((1, d), const),
                  pl.BlockSpec((ple, d), const)],
        out_specs=pl.BlockSpec((t, d), tok),
        out_shape=jax.ShapeDtypeStruct((n_tok, d), F32),
        scratch_shapes=_gather_scratch(TOP_K * t, d, y_sorted.dtype),
        compiler_params=_params(("arbitrary",), 56),
        name="combine_ln_ple",
    )(dest, y_sorted, gates, x1, p2d, ln_g, ln_b, w_gate, b_gate, w_ple)


def _routing_tables(top_e, rank, counts, n_blocks):
    n_tok = top_e.shape[0]
    rb = ROW_BLOCK
    counts = counts.reshape(-1).astype(jnp.int32)
    padded = (counts + rb - 1) // rb * rb
    pad_end = jnp.cumsum(padded)
    pad_start = pad_end - padded
    dest = pad_start[top_e] + rank
    tok = jnp.broadcast_to(jnp.arange(n_tok, dtype=jnp.int32)[:, None], dest.shape)
    spread = jnp.arange(n_blocks * rb, dtype=jnp.int32) % n_tok
    row_tok = spread.at[dest.reshape(-1)].set(tok.reshape(-1), unique_indices=True)
    block_start = jnp.arange(n_blocks, dtype=jnp.int32) * rb
    block_e = jnp.minimum(
        jnp.sum((pad_end[None, :] <= block_start[:, None]).astype(jnp.int32), axis=1),
        counts.shape[0] - 1)
    n_used = (pad_end[-1] // rb).astype(jnp.int32).reshape(1)
    block_rows = jnp.clip(counts[block_e] - (block_start - pad_start[block_e]), 0, rb)
    block_rows = (block_rows + DMA_UNROLL - 1) // DMA_UNROLL * DMA_UNROLL
    dest_tiles = dest.reshape(n_tok // TOKEN_TILE, TOKEN_TILE, TOP_K).transpose(0, 2, 1)
    return (dest, dest_tiles.reshape(n_tok // TOKEN_TILE, TOP_K * TOKEN_TILE),
            row_tok.reshape(n_blocks, rb), block_e, n_used, block_rows.astype(jnp.int32))


def _const_tables(seq):
    t = ATT_TILE
    r = np.arange(t)
    u = (r[:, None] > r[None, :]).astype(np.float32)
    dist_t = (np.arange(MOBA_BLOCK)[:, None]
              - np.arange(MOBA_QUERY_TILE)[None, :]).astype(np.float32)
    n_blk = seq // MOBA_BLOCK
    avg = np.zeros((n_blk, seq), np.float32)
    for n in range(n_blk):
        avg[n, n * MOBA_BLOCK:(n + 1) * MOBA_BLOCK] = 1.0 / MOBA_BLOCK
    tt = np.arange(TOKEN_TILE)
    tri = (tt[None, :] < tt[:, None]).astype(np.float32)
    sel = np.zeros((MXU_DIM, MXU_DIM // 2), np.float32)
    sel[2 * np.arange(MXU_DIM // 2), np.arange(MXU_DIM // 2)] = 1.0
    slopes = 2.0 ** (-8.0 * np.arange(1, N_HEADS_MOBA + 1) / N_HEADS_MOBA)
    return (jnp.asarray(u, BF16), jnp.asarray(dist_t, F32), jnp.asarray(avg, BF16),
            jnp.asarray(tri, BF16), jnp.asarray(sel, BF16),
            jnp.asarray(np.concatenate([slopes * LOG2_E, 1.0 / (slopes * LOG2_E)]), F32))


def kernel(x, p, w_in, norm_sb, norm_moba, w_out, ln1_g, ln1_b, w_router, b_router, w_up, b_up,
           w_down, b_down, ln2_g, ln2_b, w_ple, w_ple_gate, b_ple_gate):
    batch, seq, d = x.shape
    n_tok = batch * seq
    sb_width = N_HEADS_SB * HEAD_DIM
    mb_width = N_HEADS_MOBA * HEAD_DIM
    n_exp = w_router.shape[-1]
    n_blocks = -(-n_tok * TOP_K // ROW_BLOCK) + n_exp
    u, dist_t, avg, tri, sel, slopes2 = _const_tables(seq)
    heads = sb_width // HEAD_DIM
    x2d = x.reshape(n_tok, d)
    q_scales = ((0, 1.0 / math.sqrt(HEAD_DIM)),
                (3 * sb_width // PROJ_TN, LOG2_E / math.sqrt(HEAD_DIM)))

    for i in range(DEPTH):
        qkv = _qkv_proj(x2d, w_in[i].astype(BF16), q_scales).reshape(batch, seq, -1)
        mixed_sb = _sb_attention(qkv, u, norm_sb[i].reshape(1, -1),
                                 q_col=0, k_col=heads, v_col=2 * heads, n_heads=N_HEADS_SB)
        mixed_mb = _moba_attention(qkv, slopes2, avg, dist_t, norm_moba[i].reshape(1, -1),
                                   q_col=3 * heads, k_col=3 * heads + N_HEADS_MOBA,
                                   v_col=3 * heads + 2 * N_HEADS_MOBA, n_heads=N_HEADS_MOBA)
        x1, top_e, gates, rank, counts = _oproj_router(
            mixed_sb.reshape(n_tok, sb_width), mixed_mb.reshape(n_tok, mb_width), x2d,
            w_out[i].astype(BF16), ln1_g[i].reshape(1, -1), ln1_b[i].reshape(1, -1),
            w_router[i], b_router[i].reshape(1, -1), tri)
        _, dest_tiles, row_tok, block_e, n_used, block_rows = _routing_tables(
            top_e, rank, counts, n_blocks)
        xs = _sort_gather(block_rows, row_tok, x1, n_blocks)
        act = _expert_up(block_e, n_used, xs, w_up[i], b_up[i].reshape(n_exp, 1, -1), sel, n_blocks)
        y_sorted = _expert_down(block_e, n_used, act, w_down[i], b_down[i].reshape(n_exp, 1, -1),
                                n_blocks)
        x2d = _combine(dest_tiles, y_sorted, gates, x1,
                       p[i].reshape(n_tok, -1), ln2_g[i].reshape(1, -1), ln2_b[i].reshape(1, -1),
                       w_ple_gate[i].astype(BF16), b_ple_gate[i].reshape(1, -1),
                       w_ple[i].astype(BF16))
    return x2d.reshape(batch, seq, d)
```

```python
import functools
import math

import jax
import jax.numpy as jnp
import numpy as np
from jax import lax
from jax.experimental import pallas as pl
from jax.experimental.pallas import tpu as pltpu

HEAD_DIM = 128
N_HEADS_SB = 8
N_HEADS_MOBA = 8
MOBA_BLOCK = 256
MOBA_TOPK = 3
N_EXPERTS = 32
TOP_K = 4
SWIGLU_LIMIT = 7.0
SWIGLU_ALPHA = 1.702
LN_EPS = 1e-5
RMS_EPS = 1e-6
NEG_INF = -1e30
DEPTH = 1
DEEPNORM_ALPHA = (2.0 * DEPTH) ** 0.25
LOG2_E = math.log2(math.e)
EXP_UNDERFLOW = 110.0
EXP2_UNDERFLOW = 160.0

ATT_TILE = 256
MOBA_QUERY_TILE = 512
MOBA_GROUP = 2
SB_QUERY_TILE = 512
SB_GROUP = 1
ROW_BLOCK = 512
TOKEN_TILE = 256
PROJ_TM = 1024
PROJ_TN = 1024
EXPERT_TN = 1024
MXU_DIM = 256
DMA_UNROLL = 8
MIB = 1024 * 1024

F32 = jnp.float32
BF16 = jnp.bfloat16


def _params(semantics, vmem_mib):
    return pltpu.CompilerParams(dimension_semantics=semantics,
                                vmem_limit_bytes=vmem_mib * MIB)


def _split_bf16(a):
    hi = a.astype(BF16)
    lo = (a - hi.astype(F32)).astype(BF16)
    return hi, lo


def _dot(a, b):
    return jnp.dot(a, b, preferred_element_type=F32)


def _dot_nt(a, b):
    return lax.dot_general(a, b, (((1,), (1,)), ((), ())), preferred_element_type=F32)


def _qkv_kernel(x_ref, w_ref, o_ref, xb_ref, *, tile_scales):
    n = pl.program_id(1)

    @pl.when(n == 0)
    def _():
        xb_ref[...] = x_ref[...].astype(BF16)

    scale = jnp.float32(1.0)
    for tile, s in tile_scales:
        scale = jnp.where(n == tile, s, scale)
    o_ref[...] = (_dot(xb_ref[...], w_ref[...]) * scale).astype(o_ref.dtype)


def _qkv_proj(x2d, w_bf16, tile_scales):
    n_tok, d = x2d.shape
    n_out = w_bf16.shape[1]
    tm = min(PROJ_TM, n_tok)
    tn = PROJ_TN
    return pl.pallas_call(
        functools.partial(_qkv_kernel, tile_scales=tile_scales),
        grid=(n_tok // tm, n_out // tn),
        in_specs=[pl.BlockSpec((tm, d), lambda m, n: (m, 0)),
                  pl.BlockSpec((d, tn), lambda m, n: (0, n))],
        out_specs=pl.BlockSpec((tm, tn), lambda m, n: (m, n)),
        out_shape=jax.ShapeDtypeStruct((n_tok, n_out), BF16),
        scratch_shapes=[pltpu.VMEM((tm, d), BF16)],
        compiler_params=_params(("arbitrary", "arbitrary"), 48),
        name="qkv_proj",
    )(x2d, w_bf16)


def _head_rms(o, gain):
    ms = jnp.mean(o * o, axis=-1, keepdims=True)
    return o * lax.rsqrt(ms + RMS_EPS) * gain


def _head_rms_t(o_t, gain):
    ms = jnp.mean(o_t * o_t, axis=0, keepdims=True)
    return (o_t * lax.rsqrt(ms + RMS_EPS)).T * gain


def _store_transposed_values(v_ref, vt_ref, t):
    def body(n, carry):
        v = v_ref[pl.ds(pl.multiple_of(n * t, t), t), :]
        vt_ref[n] = v.astype(F32).T.astype(vt_ref.dtype)
        return carry

    lax.fori_loop(0, vt_ref.shape[0], body, 0)


def _sb_kernel(q_ref, k_ref, v_ref, u_ref, g_ref, o_ref, knorm_ref, *, tq, t, group):
    qi = pl.program_id(2)
    ratio = tq // t

    @pl.when(qi == 0)
    def _():
        def body(n, m):
            k = k_ref[pl.ds(pl.multiple_of(n * t, t), t), :].astype(F32)
            return jnp.maximum(m, jnp.max(jnp.sum(k * k, axis=1, keepdims=True), axis=0, keepdims=True))

        knorm_ref[...] = jnp.sqrt(lax.fori_loop(0, k_ref.shape[0] // t, body, jnp.zeros((1, 1), F32)))

    q = q_ref[...]
    u = u_ref[...]
    qry = lax.broadcasted_iota(jnp.int32, (tq, t), 0)
    key = lax.broadcasted_iota(jnp.int32, (tq, t), 1)
    qf = q.astype(F32)
    z_bound = jnp.sqrt(jnp.sum(qf * qf, axis=1, keepdims=True)) * knorm_ref[...] * 1.001 + 1e-3

    def local(j, past):
        k = k_ref[pl.ds(pl.multiple_of(j * t, t), t), :]
        z = _dot_nt(q, k)
        drop = jnp.maximum(z, 0.0) + jnp.log(1.0 + jnp.exp(-jnp.abs(z)))
        if past is not None:
            drop = jnp.where(past, drop, 0.0)
        hi, lo = _split_bf16(drop)
        between = _dot(hi, u) + _dot(lo, u)
        first = slice(0, HEAD_DIM)
        total = (between[:, first] + drop[:, first])[:, :1]
        return z, drop + between, total

    def finish(j, z, spent, c, acc, past):
        v = v_ref[pl.ds(pl.multiple_of(j * t, t), t), :]
        w = jnp.exp(z - (spent + c))
        if past is not None:
            w = jnp.where(past, w, 0.0)
        return acc + _dot(w.astype(BF16), v)

    def sweep(tiles, c, acc):
        parts = [(j, past) + local(j, past) for j, past in tiles]
        for j, past, z, spent, total in parts:
            acc = finish(j, z, spent, c, acc, past)
            c = c + total
        return c, acc

    diagonal = [(qi * ratio + d, key + d * t < qry) for d in reversed(range(ratio))]
    carry = sweep(diagonal, jnp.zeros((tq, 1), F32), jnp.zeros((tq, HEAD_DIM), F32))

    def live(c):
        return jnp.min(c - z_bound) <= EXP_UNDERFLOW

    def cond(state):
        it, go, _, _ = state
        return jnp.logical_and(it < qi * (ratio // group), go)

    def body(state):
        it, _, c, acc = state
        first = qi * ratio - 1 - it * group
        c, acc = sweep([(first - g, None) for g in range(group)], c, acc)
        return it + 1, live(c), c, acc

    _, _, _, acc = lax.while_loop(cond, body, (jnp.int32(0), live(carry[0])) + carry)
    o_ref[...] = _head_rms(acc, g_ref[...]).astype(o_ref.dtype)


def _sb_attention(qkv, u, gain, *, q_col, k_col, v_col, n_heads):
    b, s, _ = qkv.shape
    t = ATT_TILE
    tq = SB_QUERY_TILE
    assert s % tq == 0 and tq % t == 0 and (tq // t) % SB_GROUP == 0
    return pl.pallas_call(
        functools.partial(_sb_kernel, tq=tq, t=t, group=SB_GROUP),
        grid=(b, n_heads, s // tq),
        in_specs=[pl.BlockSpec((None, tq, HEAD_DIM), lambda b, h, i: (b, i, q_col + h)),
                  pl.BlockSpec((None, s, HEAD_DIM), lambda b, h, i: (b, 0, k_col + h)),
                  pl.BlockSpec((None, s, HEAD_DIM), lambda b, h, i: (b, 0, v_col + h)),
                  pl.BlockSpec((t, t), lambda b, h, i: (0, 0)),
                  pl.BlockSpec((1, HEAD_DIM), lambda b, h, i: (0, h))],
        out_specs=pl.BlockSpec((None, tq, HEAD_DIM), lambda b, h, i: (b, i, h)),
        out_shape=jax.ShapeDtypeStruct((b, s, n_heads * HEAD_DIM), BF16),
        scratch_shapes=[pltpu.VMEM((1, 1), F32)],
        compiler_params=_params(("arbitrary", "arbitrary", "arbitrary"), 32),
        name="sb_attention",
    )(qkv, qkv, qkv, u, gain)


def _moba_kernel(slope_ref, q_ref, k_ref, v_ref, avg_ref, dist_ref, g_ref, o_ref,
                 km_ref, vt_ref, bias_ref, sc_ref, knorm_ref, *, tq, t, n_blk, group):
    h = pl.program_id(1)
    qi = pl.program_id(2)
    ratio = tq // t

    @pl.when(qi == 0)
    def _():
        km_ref[...] = _dot(avg_ref[...], k_ref[...])
        _store_transposed_values(v_ref, vt_ref, t)

        def body(n, m):
            k = k_ref[pl.ds(pl.multiple_of(n * t, t), t), :].astype(F32)
            return jnp.maximum(m, jnp.max(jnp.sum(k * k, axis=1, keepdims=True), axis=0, keepdims=True))

        knorm_ref[...] = jnp.sqrt(lax.fori_loop(0, n_blk, body, jnp.zeros((1, 1), F32)))

    q = q_ref[...]
    slope = slope_ref[h]
    n_heads = slope_ref.shape[0] // 2
    inv_slope = slope_ref[n_heads + h]
    alibi = slope * dist_ref[...]
    key = lax.broadcasted_iota(jnp.int32, (t, tq), 0)
    qry = lax.broadcasted_iota(jnp.int32, (t, tq), 1)
    lane = lax.broadcasted_iota(jnp.int32, (1, tq), 1)
    sub = functools.reduce(jnp.add, [(lane >= d * t).astype(jnp.int32) for d in range(1, ratio)])
    own = qi * ratio + sub

    def tile(j):
        k = k_ref[pl.ds(pl.multiple_of(j * t, t), t), :]
        return _dot_nt(k, q) + alibi - slope * (qi * tq - j * t).astype(F32)

    km_hi, km_lo = _split_bf16(km_ref[...])
    gate = _dot_nt(km_hi, q) + _dot_nt(km_lo, q)
    blk = lax.broadcasted_iota(jnp.int32, (n_blk, tq), 0)
    valid = blk < own
    work = jnp.where(valid, gate, -jnp.inf)
    sel = jnp.zeros((n_blk, tq), jnp.bool_)
    for _ in range(MOBA_TOPK):
        m = jnp.max(work, axis=0, keepdims=True)
        idx = jnp.min(jnp.where(work == m, blk, n_blk), axis=0, keepdims=True)
        hit = blk == idx
        sel = jnp.logical_or(sel, hit)
        work = jnp.where(hit, -jnp.inf, work)
    bias_ref[...] = jnp.where(jnp.logical_and(sel, valid), 0.0, NEG_INF)

    scores = []
    for d in range(ratio):
        j = qi * ratio + d
        mask = jnp.where(sub > d, bias_ref[pl.ds(j, 1), :],
                         jnp.where(jnp.logical_and(sub == d, key + d * t <= qry), 0.0, NEG_INF))
        scores.append(tile(j) + mask)
    m0 = functools.reduce(jnp.maximum, [jnp.max(s, axis=0, keepdims=True) for s in scores])
    l0 = jnp.zeros((1, tq), F32)
    acc0 = jnp.zeros((HEAD_DIM, tq), F32)
    for d, s in enumerate(scores):
        p = jnp.exp2(s - m0)
        l0 = l0 + jnp.sum(p, axis=0, keepdims=True)
        acc0 = acc0 + _dot(vt_ref[qi * ratio + d], p.astype(BF16))

    n_groups = qi * (ratio // group)
    dead = jnp.where(n_groups > 0, 0.0, NEG_INF)

    def store_scores(grp, slot):
        for g in range(group):
            j = grp * group + g
            sc_ref[slot, g] = tile(j) + (bias_ref[pl.ds(j, 1), :] + dead)

    def fold(grp, slot, carry):
        m_run, l_run, acc = carry
        scores = [sc_ref[slot, g] for g in range(group)]
        m_new = functools.reduce(
            jnp.maximum, [m_run] + [jnp.max(s, axis=0, keepdims=True) for s in scores])
        a = jnp.exp2(m_run - m_new)
        l_run = a * l_run
        acc = a * acc
        for g, s in enumerate(scores):
            p = jnp.exp2(s - m_new)
            l_run = l_run + jnp.sum(p, axis=0, keepdims=True)
            acc = acc + _dot(vt_ref[grp * group + g], p.astype(BF16))
        return m_new, l_run, acc

    def body(it, carry):
        carry = fold(it, lax.rem(it, 2), carry)
        store_scores(it + 1, lax.rem(it + 1, 2))
        return carry

    qf = q.astype(F32)
    z_bound = (jnp.sqrt(jnp.max(jnp.sum(qf * qf, axis=1, keepdims=True), axis=0, keepdims=True))
               * knorm_ref[...] * 1.001 + 1e-3)
    reach = (EXP2_UNDERFLOW + 2.0 * z_bound) * inv_slope
    skip = jnp.floor(((qi * tq).astype(F32) - reach) * (1.0 / (t * group))).astype(jnp.int32)
    last = jnp.maximum(n_groups - 1, 0)
    first = jnp.clip(skip[0, 0], 0, last)
    store_scores(first, lax.rem(first, 2))
    carry = lax.fori_loop(first, last, body, (m0, l0, acc0))
    _, l_run, acc = fold(last, lax.rem(last, 2), carry)
    o_ref[...] = _head_rms_t(acc / l_run, g_ref[...]).astype(o_ref.dtype)


def _moba_attention(qkv, slopes, avg, dist_t, gain, *, q_col, k_col, v_col, n_heads):
    b, s, _ = qkv.shape
    t = MOBA_BLOCK
    tq = MOBA_QUERY_TILE
    group = MOBA_GROUP
    n_blk = s // t
    assert s % tq == 0 and tq % t == 0 and (tq // t) % group == 0
    grid_spec = pltpu.PrefetchScalarGridSpec(
        num_scalar_prefetch=1,
        grid=(b, n_heads, s // tq),
        in_specs=[pl.BlockSpec((None, tq, HEAD_DIM), lambda b, h, i, sl: (b, i, q_col + h)),
                  pl.BlockSpec((None, s, HEAD_DIM), lambda b, h, i, sl: (b, 0, k_col + h)),
                  pl.BlockSpec((None, s, HEAD_DIM), lambda b, h, i, sl: (b, 0, v_col + h)),
                  pl.BlockSpec((n_blk, s), lambda b, h, i, sl: (0, 0)),
                  pl.BlockSpec((t, tq), lambda b, h, i, sl: (0, 0)),
                  pl.BlockSpec((1, HEAD_DIM), lambda b, h, i, sl: (0, h))],
        out_specs=pl.BlockSpec((None, tq, HEAD_DIM), lambda b, h, i, sl: (b, i, h)),
        scratch_shapes=[pltpu.VMEM((n_blk, HEAD_DIM), F32),
                        pltpu.VMEM((n_blk, HEAD_DIM, t), BF16),
                        pltpu.VMEM((n_blk, tq), F32),
                        pltpu.VMEM((2, group, t, tq), F32),
                        pltpu.VMEM((1, 1), F32)])
    return pl.pallas_call(
        functools.partial(_moba_kernel, tq=tq, t=t, n_blk=n_blk, group=group),
        grid_spec=grid_spec,
        out_shape=jax.ShapeDtypeStruct((b, s, n_heads * HEAD_DIM), BF16),
        compiler_params=_params(("arbitrary", "arbitrary", "arbitrary"), 32),
        name="moba_attention",
    )(slopes, qkv, qkv, qkv, avg, dist_t, gain)


def _layer_norm(x, g, b):
    mu = jnp.mean(x, axis=-1, keepdims=True)
    xc = x - mu
    var = jnp.mean(xc * xc, axis=-1, keepdims=True)
    return xc * lax.rsqrt(var + LN_EPS) * g + b


def _oproj_router_kernel(ms_ref, mm_ref, x_ref, wo_ref, g_ref, b_ref, wr_ref, br_ref, tri_ref,
                         x1_ref, te_ref, gt_ref, rk_ref, cnt_ref, carry_ref, *, sb_width):
    i = pl.program_id(0)

    @pl.when(i == 0)
    def _():
        carry_ref[...] = jnp.zeros_like(carry_ref)

    y = _dot(ms_ref[...], wo_ref[:sb_width, :]) + _dot(mm_ref[...], wo_ref[sb_width:, :])
    x1 = _layer_norm(DEEPNORM_ALPHA * x_ref[...] + y, g_ref[...], b_ref[...])
    x1_ref[...] = x1

    x_hi, x_lo = _split_bf16(x1)
    w_hi, w_lo = _split_bf16(wr_ref[...])
    logits = _dot(x_hi, w_hi) + _dot(x_hi, w_lo) + _dot(x_lo, w_hi) + br_ref[...]

    n_tok, n_exp = logits.shape
    lane = lax.broadcasted_iota(jnp.int32, (n_tok, n_exp), 1)
    work = logits
    vals, hits = [], []
    for _ in range(TOP_K):
        m = jnp.max(work, axis=1, keepdims=True)
        idx = jnp.min(jnp.where(work == m, lane, n_exp), axis=1, keepdims=True)
        hit = lane == idx
        vals.append(m)
        hits.append(hit)
        work = jnp.where(hit, -jnp.inf, work)
    exps = [jnp.exp(v - vals[0]) for v in vals]
    denom = functools.reduce(jnp.add, exps)

    onehot = functools.reduce(jnp.add, [hit.astype(F32) for hit in hits])
    rank_excl = carry_ref[...] + _dot(tri_ref[...], onehot.astype(BF16))
    carry_ref[...] = carry_ref[...] + jnp.sum(onehot, axis=0, keepdims=True)
    cnt_ref[...] = carry_ref[...]

    te_ref[...] = jnp.concatenate(
        [jnp.sum(jnp.where(hit, lane, 0), axis=1, keepdims=True) for hit in hits], axis=1)
    gt_ref[...] = jnp.concatenate([e / denom for e in exps], axis=1)
    rk_ref[...] = jnp.concatenate(
        [jnp.sum(jnp.where(hit, rank_excl, 0.0), axis=1, keepdims=True) for hit in hits],
        axis=1).astype(jnp.int32)


def _oproj_router(mixed_sb, mixed_mb, x2d, w_out, ln_g, ln_b, w_router, b_router, tri):
    n_tok, d = x2d.shape
    sb_width = mixed_sb.shape[1]
    mb_width = mixed_mb.shape[1]
    n_exp = w_router.shape[1]
    t = TOKEN_TILE
    const = lambda i: (0, 0)
    tok = lambda i: (i, 0)
    return pl.pallas_call(
        functools.partial(_oproj_router_kernel, sb_width=sb_width),
        grid=(n_tok // t,),
        in_specs=[pl.BlockSpec((t, sb_width), tok),
                  pl.BlockSpec((t, mb_width), tok),
                  pl.BlockSpec((t, d), tok),
                  pl.BlockSpec((sb_width + mb_width, d), const),
                  pl.BlockSpec((1, d), const),
                  pl.BlockSpec((1, d), const),
                  pl.BlockSpec((d, n_exp), const),
                  pl.BlockSpec((1, n_exp), const),
                  pl.BlockSpec((t, t), const)],
        out_specs=[pl.BlockSpec((t, d), tok),
                   pl.BlockSpec((t, TOP_K), tok),
                   pl.BlockSpec((t, TOP_K), tok),
                   pl.BlockSpec((t, TOP_K), tok),
                   pl.BlockSpec((1, n_exp), const)],
        out_shape=[jax.ShapeDtypeStruct((n_tok, d), F32),
                   jax.ShapeDtypeStruct((n_tok, TOP_K), jnp.int32),
                   jax.ShapeDtypeStruct((n_tok, TOP_K), F32),
                   jax.ShapeDtypeStruct((n_tok, TOP_K), jnp.int32),
                   jax.ShapeDtypeStruct((1, n_exp), F32)],
        scratch_shapes=[pltpu.VMEM((1, n_exp), F32)],
        compiler_params=_params(("arbitrary",), 48),
        name="oproj_ln_router",
    )(mixed_sb, mixed_mb, x2d, w_out, ln_g, ln_b, w_router, b_router, tri)


def _issue_row_copies(n_rows, make_copy):
    def body(it, carry):
        for u in range(DMA_UNROLL):
            make_copy(it * DMA_UNROLL + u).start(priority=u % 2)
        return carry

    lax.fori_loop(0, n_rows // DMA_UNROLL, body, 0)


def _gathered_rows(step, n_steps, idx_hbm, src_hbm, idx_ref, rows_ref, idx_sems, row_sems,
                   count=None):
    n_rows = rows_ref.shape[1]

    def idx_copy(s):
        slot = lax.rem(s, 2)
        return pltpu.make_async_copy(idx_hbm.at[s], idx_ref.at[slot], idx_sems.at[slot])

    def issue(s):
        slot = lax.rem(s, 2)
        idx_copy(s).wait()
        _issue_row_copies(n_rows if count is None else count(s), lambda r: pltpu.make_async_copy(
            src_hbm.at[pl.ds(idx_ref[slot, r], 1)], rows_ref.at[slot, pl.ds(r, 1)],
            row_sems.at[slot]))

    @pl.when(step == 0)
    def _():
        idx_copy(0).start()
        issue(0)
        if n_steps > 1:
            idx_copy(1).start()

    @pl.when(step + 1 < n_steps)
    def _():
        issue(step + 1)

        @pl.when(step + 2 < n_steps)
        def _():
            idx_copy(step + 2).start()

    slot = lax.rem(step, 2)
    if count is None:
        pltpu.make_async_copy(src_hbm.at[pl.ds(0, n_rows)], rows_ref.at[slot],
                              row_sems.at[slot]).wait()
    else:
        n = pl.multiple_of(count(step), DMA_UNROLL)

        @pl.when(n > 0)
        def _():
            pltpu.make_async_copy(src_hbm.at[pl.ds(0, n)], rows_ref.at[slot, pl.ds(0, n)],
                                  row_sems.at[slot]).wait()
    return slot


def _gather_scratch(n_rows, width, dtype):
    return [pltpu.SMEM((2, n_rows), jnp.int32),
            pltpu.VMEM((2, n_rows, width), dtype),
            pltpu.SemaphoreType.DMA((2,)),
            pltpu.SemaphoreType.DMA((2,))]


def _sort_gather_kernel(count_ref, tok_hbm, x_hbm, o_ref, idx_ref, rows_ref, idx_sems, row_sems,
                        *, n_steps):
    b = pl.program_id(0)

    @pl.when(b == 0)
    def _():
        rows_ref[...] = jnp.zeros_like(rows_ref)

    slot = _gathered_rows(b, n_steps, tok_hbm, x_hbm, idx_ref, rows_ref, idx_sems, row_sems,
                          count=lambda s: count_ref[s])
    o_ref[...] = rows_ref[slot].astype(o_ref.dtype)


def _sort_gather(block_rows, row_tok, x1, n_blocks):
    n_tok, d = x1.shape
    rb = ROW_BLOCK
    grid_spec = pltpu.PrefetchScalarGridSpec(
        num_scalar_prefetch=1,
        grid=(n_blocks,),
        in_specs=[pl.BlockSpec(memory_space=pl.ANY), pl.BlockSpec(memory_space=pl.ANY)],
        out_specs=pl.BlockSpec((rb, d), lambda b, cnt: (b, 0)),
        scratch_shapes=_gather_scratch(rb, d, x1.dtype))
    return pl.pallas_call(
        functools.partial(_sort_gather_kernel, n_steps=n_blocks),
        grid_spec=grid_spec,
        out_shape=jax.ShapeDtypeStruct((n_blocks * rb, d), BF16),
        compiler_params=_params(("arbitrary",), 24),
        name="sort_gather",
    )(block_rows, row_tok, x1)


def _expert_changed(be_ref, b):
    prev = be_ref[jnp.maximum(b - 1, 0)]
    return jnp.logical_or(b == 0, be_ref[b] != prev)


def _expert_up_kernel(be_ref, nused_ref, xs_ref, w_ref, bias_ref, sel_ref, o_ref, wb_ref):
    b = pl.program_id(1)

    @pl.when(b >= nused_ref[0])
    def _():
        o_ref[...] = jnp.zeros_like(o_ref)

    @pl.when(b < nused_ref[0])
    def _():
        @pl.when(_expert_changed(be_ref, b))
        def _():
            wb_ref[...] = w_ref[...].astype(BF16)

        hdn = _dot(xs_ref[...], wb_ref[...]) + bias_ref[...]
        tn = hdn.shape[1]
        nxt = pltpu.roll(hdn, tn - 1, 1)
        glu = jnp.minimum(hdn, SWIGLU_LIMIT)
        lin = jnp.clip(nxt, -SWIGLU_LIMIT, SWIGLU_LIMIT)
        act = (glu * jax.nn.sigmoid(SWIGLU_ALPHA * glu) * (lin + 1.0)).astype(BF16)
        sel = sel_ref[...]
        o_ref[...] = jnp.concatenate(
            [_dot(act[:, c:c + MXU_DIM], sel) for c in range(0, tn, MXU_DIM)], axis=1).astype(o_ref.dtype)


def _expert_up(block_e, n_used, xs, w_up, b_up, sel, n_blocks):
    n_rows, d = xs.shape
    n_exp, _, f2 = w_up.shape
    rb = ROW_BLOCK
    tn = EXPERT_TN

    def used(b, nu):
        return jnp.minimum(b, nu[0] - 1)

    grid_spec = pltpu.PrefetchScalarGridSpec(
        num_scalar_prefetch=2,
        grid=(f2 // tn, n_blocks),
        in_specs=[pl.BlockSpec((rb, d), lambda n, b, be, nu: (used(b, nu), 0)),
                  pl.BlockSpec((None, d, tn), lambda n, b, be, nu: (be[used(b, nu)], 0, n)),
                  pl.BlockSpec((None, 1, tn), lambda n, b, be, nu: (be[used(b, nu)], 0, n)),
                  pl.BlockSpec((MXU_DIM, MXU_DIM // 2), lambda n, b, be, nu: (0, 0))],
        out_specs=pl.BlockSpec((rb, tn // 2), lambda n, b, be, nu: (b, n)),
        scratch_shapes=[pltpu.VMEM((d, tn), BF16)])
    return pl.pallas_call(
        _expert_up_kernel,
        grid_spec=grid_spec,
        out_shape=jax.ShapeDtypeStruct((n_rows, f2 // 2), BF16),
        compiler_params=_params(("arbitrary", "arbitrary"), 48),
        name="expert_up",
    )(block_e, n_used, xs, w_up, b_up, sel)


def _expert_down_kernel(be_ref, nused_ref, a_ref, w_ref, bias_ref, o_ref, wb_ref):
    b = pl.program_id(1)

    @pl.when(b >= nused_ref[0])
    def _():
        o_ref[...] = jnp.zeros_like(o_ref)

    @pl.when(b < nused_ref[0])
    def _():
        @pl.when(_expert_changed(be_ref, b))
        def _():
            wb_ref[...] = w_ref[...].astype(BF16)

        o_ref[...] = _dot(a_ref[...], wb_ref[...]) + bias_ref[...]


def _expert_down(block_e, n_used, act, w_down, b_down, n_blocks):
    n_exp, f, d = w_down.shape
    rb = ROW_BLOCK
    tn = EXPERT_TN

    def used(b, nu):
        return jnp.minimum(b, nu[0] - 1)

    grid_spec = pltpu.PrefetchScalarGridSpec(
        num_scalar_prefetch=2,
        grid=(d // tn, n_blocks),
        in_specs=[pl.BlockSpec((rb, f), lambda n, b, be, nu: (used(b, nu), 0)),
                  pl.BlockSpec((None, f, tn), lambda n, b, be, nu: (be[used(b, nu)], 0, n)),
                  pl.BlockSpec((None, 1, tn), lambda n, b, be, nu: (be[used(b, nu)], 0, n))],
        out_specs=pl.BlockSpec((rb, tn), lambda n, b, be, nu: (b, n)),
        scratch_shapes=[pltpu.VMEM((f, tn), BF16)])
    return pl.pallas_call(
        _expert_down_kernel,
        grid_spec=grid_spec,
        out_shape=jax.ShapeDtypeStruct((n_blocks * rb, d), F32),
        compiler_params=_params(("arbitrary", "arbitrary"), 48),
        name="expert_down",
    )(block_e, n_used, act, w_down, b_down)


def _combine_kernel(dest_hbm, y_hbm, gate_ref, x1_ref, p_ref, g_ref, b_ref, wg_ref, bg_ref, wp_ref,
                    o_ref, idx_ref, rows_ref, idx_sems, row_sems, *, t, n_steps):
    slot = _gathered_rows(pl.program_id(0), n_steps, dest_hbm, y_hbm,
                          idx_ref, rows_ref, idx_sems, row_sems)
    gates = gate_ref[...]
    f = functools.reduce(
        jnp.add, [gates[:, k:k + 1] * rows_ref[slot, pl.ds(k * t, t), :] for k in range(TOP_K)])
    x2 = _layer_norm(DEEPNORM_ALPHA * x1_ref[...] + f, g_ref[...], b_ref[...])
    gate = jax.nn.sigmoid(_dot(x2.astype(BF16), wg_ref[...]) + bg_ref[...])
    emb = _dot(p_ref[...].astype(BF16), wp_ref[...])
    o_ref[...] = x2 + gate * emb


def _combine(dest, y_sorted, gates, x1, p2d, ln_g, ln_b, w_gate, b_gate, w_ple):
    n_tok, d = x1.shape
    ple = p2d.shape[1]
    t = TOKEN_TILE
    const = lambda i: (0, 0)
    tok = lambda i: (i, 0)
    return pl.pallas_call(
        functools.partial(_combine_kernel, t=t, n_steps=n_tok // t),
        grid=(n_tok // t,),
        in_specs=[pl.BlockSpec(memory_space=pl.ANY),
                  pl.BlockSpec(memory_space=pl.ANY),
                  pl.BlockSpec((t, TOP_K), tok),
                  pl.BlockSpec((t, d), tok),
                  pl.BlockSpec((t, ple), tok),
                  pl.BlockSpec((1, d), const),
                  pl.BlockSpec((1, d), const),
                  pl.BlockSpec((d, d), const),
                  pl.BlockSpec((1, d), const),
                  pl.BlockSpec((ple, d), const)],
        out_specs=pl.BlockSpec((t, d), tok),
        out_shape=jax.ShapeDtypeStruct((n_tok, d), F32),
        scratch_shapes=_gather_scratch(TOP_K * t, d, y_sorted.dtype),
        compiler_params=_params(("arbitrary",), 56),
        name="combine_ln_ple",
    )(dest, y_sorted, gates, x1, p2d, ln_g, ln_b, w_gate, b_gate, w_ple)


def _routing_tables(top_e, rank, counts, n_blocks):
    n_tok = top_e.shape[0]
    rb = ROW_BLOCK
    counts = counts.reshape(-1).astype(jnp.int32)
    padded = (counts + rb - 1) // rb * rb
    pad_end = jnp.cumsum(padded)
    pad_start = pad_end - padded
    dest = pad_start[top_e] + rank
    tok = jnp.broadcast_to(jnp.arange(n_tok, dtype=jnp.int32)[:, None], dest.shape)
    spread = jnp.arange(n_blocks * rb, dtype=jnp.int32) % n_tok
    row_tok = spread.at[dest.reshape(-1)].set(tok.reshape(-1), unique_indices=True)
    block_start = jnp.arange(n_blocks, dtype=jnp.int32) * rb
    block_e = jnp.minimum(
        jnp.sum((pad_end[None, :] <= block_start[:, None]).astype(jnp.int32), axis=1),
        counts.shape[0] - 1)
    n_used = (pad_end[-1] // rb).astype(jnp.int32).reshape(1)
    block_rows = jnp.clip(counts[block_e] - (block_start - pad_start[block_e]), 0, rb)
    block_rows = (block_rows + DMA_UNROLL - 1) // DMA_UNROLL * DMA_UNROLL
    dest_tiles = dest.reshape(n_tok // TOKEN_TILE, TOKEN_TILE, TOP_K).transpose(0, 2, 1)
    return (dest, dest_tiles.reshape(n_tok // TOKEN_TILE, TOP_K * TOKEN_TILE),
            row_tok.reshape(n_blocks, rb), block_e, n_used, block_rows.astype(jnp.int32))


def _const_tables(seq):
    t = ATT_TILE
    r = np.arange(t)
    u = (r[:, None] > r[None, :]).astype(np.float32)
    dist_t = (np.arange(MOBA_BLOCK)[:, None]
              - np.arange(MOBA_QUERY_TILE)[None, :]).astype(np.float32)
    n_blk = seq // MOBA_BLOCK
    avg = np.zeros((n_blk, seq), np.float32)
    for n in range(n_blk):
        avg[n, n * MOBA_BLOCK:(n + 1) * MOBA_BLOCK] = 1.0 / MOBA_BLOCK
    tt = np.arange(TOKEN_TILE)
    tri = (tt[None, :] < tt[:, None]).astype(np.float32)
    sel = np.zeros((MXU_DIM, MXU_DIM // 2), np.float32)
    sel[2 * np.arange(MXU_DIM // 2), np.arange(MXU_DIM // 2)] = 1.0
    slopes = 2.0 ** (-8.0 * np.arange(1, N_HEADS_MOBA + 1) / N_HEADS_MOBA)
    return (jnp.asarray(u, BF16), jnp.asarray(dist_t, F32), jnp.asarray(avg, BF16),
            jnp.asarray(tri, BF16), jnp.asarray(sel, BF16),
            jnp.asarray(np.concatenate([slopes * LOG2_E, 1.0 / (slopes * LOG2_E)]), F32))


def kernel(x, p, w_in, norm_sb, norm_moba, w_out, ln1_g, ln1_b, w_router, b_router, w_up, b_up,
           w_down, b_down, ln2_g, ln2_b, w_ple, w_ple_gate, b_ple_gate):
    batch, seq, d = x.shape
    n_tok = batch * seq
    sb_width = N_HEADS_SB * HEAD_DIM
    mb_width = N_HEADS_MOBA * HEAD_DIM
    n_exp = w_router.shape[-1]
    n_blocks = -(-n_tok * TOP_K // ROW_BLOCK) + n_exp
    u, dist_t, avg, tri, sel, slopes2 = _const_tables(seq)
    heads = sb_width // HEAD_DIM
    x2d = x.reshape(n_tok, d)
    q_scales = ((0, 1.0 / math.sqrt(HEAD_DIM)),
                (3 * sb_width // PROJ_TN, LOG2_E / math.sqrt(HEAD_DIM)))

    for i in range(DEPTH):
        qkv = _qkv_proj(x2d, w_in[i].astype(BF16), q_scales).reshape(batch, seq, -1)
        mixed_sb = _sb_attention(qkv, u, norm_sb[i].reshape(1, -1),
                                 q_col=0, k_col=heads, v_col=2 * heads, n_heads=N_HEADS_SB)
        mixed_mb = _moba_attention(qkv, slopes2, avg, dist_t, norm_moba[i].reshape(1, -1),
                                   q_col=3 * heads, k_col=3 * heads + N_HEADS_MOBA,
                                   v_col=3 * heads + 2 * N_HEADS_MOBA, n_heads=N_HEADS_MOBA)
        x1, top_e, gates, rank, counts = _oproj_router(
            mixed_sb.reshape(n_tok, sb_width), mixed_mb.reshape(n_tok, mb_width), x2d,
            w_out[i].astype(BF16), ln1_g[i].reshape(1, -1), ln1_b[i].reshape(1, -1),
            w_router[i], b_router[i].reshape(1, -1), tri)
        _, dest_tiles, row_tok, block_e, n_used, block_rows = _routing_tables(
            top_e, rank, counts, n_blocks)
        xs = _sort_gather(block_rows, row_tok, x1, n_blocks)
        act = _expert_up(block_e, n_used, xs, w_up[i], b_up[i].reshape(n_exp, 1, -1), sel, n_blocks)
        y_sorted = _expert_down(block_e, n_used, act, w_down[i], b_down[i].reshape(n_exp, 1, -1),
                                n_blocks)
        x2d = _combine(dest_tiles, y_sorted, gates, x1,
                       p[i].reshape(n_tok, -1), ln2_g[i].reshape(1, -1), ln2_b[i].reshape(1, -1),
                       w_ple_gate[i].astype(BF16), b_ple_gate[i].reshape(1, -1),
                       w_ple[i].astype(BF16))
    return x2d.reshape(batch, seq, d)
```
